```python
import math
import jax
import jax.numpy as jnp
from jax import lax
import numpy as np

D_MODEL = 1024
BATCH = 8
SEQ = 4096
DEPTH = 1

CHUNK = 64
Q_BLOCK = 128
EPS = 1e-6

DA_HEADS = 8
DA_HEAD_DIM = 64
DA_V_DIM = 2 * DA_HEAD_DIM
DA_QK_WIDTH = DA_HEADS * 2 * DA_HEAD_DIM
DA_WIDTH = DA_HEADS * DA_V_DIM

SSD_HEAD_DIM = 64
SSD_WIDTH = D_MODEL
SSD_HEADS = SSD_WIDTH // SSD_HEAD_DIM
SSD_GROUPS = 4
SSD_STATE = 128
SSD_CONV = 4
SSD_XBC = SSD_WIDTH + 2 * SSD_GROUPS * SSD_STATE

N_BRANCH = 2

PEER_HEADS = 8
PEER_NKEYS = 128
PEER_EXPERTS = PEER_NKEYS * PEER_NKEYS
PEER_KEY_DIM = 256
PEER_HALF = PEER_KEY_DIM // 2
PEER_TOPK = 16
PEER_TOKEN_BLOCK = 128

COLS = (DA_QK_WIDTH, DA_QK_WIDTH, DA_WIDTH, SSD_WIDTH, SSD_XBC, SSD_HEADS, N_BRANCH * D_MODEL)
IN_WIDTH = sum(COLS)
SPLITS = tuple(sum(COLS[:i + 1]) for i in range(len(COLS) - 1))

kernel_name = "hybrid_diffattn_ssd_peer_block"


def lambda_init(layer_index):
    return 0.8 - 0.6 * math.exp(-0.3 * (layer_index - 1))


def rms_norm(x, g):
    xf = x.astype(jnp.float32)
    y = xf * lax.rsqrt(jnp.mean(xf * xf, axis=-1, keepdims=True) + EPS)
    return (y * g.astype(jnp.float32)).astype(x.dtype)


def alibi_slopes(n_heads):
    return jnp.exp2(-8.0 * jnp.arange(1, n_heads + 1, dtype=jnp.float32) / n_heads)


def diff_attention(q, k, v, lam):
    b, s, nh, _, d = q.shape
    nq = s // Q_BLOCK
    q_blocks = q.reshape(b, nq, Q_BLOCK, nh, 2, d).transpose(1, 0, 3, 4, 2, 5)
    k_t = k.transpose(0, 2, 3, 1, 4)
    v_t = v.transpose(0, 2, 1, 3)
    slopes = alibi_slopes(nh)
    k_pos = jnp.arange(s)
    scale = d ** -0.5

    def attend_block(args):
        q_blk, blk = args
        q_pos = blk * Q_BLOCK + jnp.arange(Q_BLOCK)
        scores = jnp.einsum('bhmqd,bhmkd->bhmqk', q_blk, k_t).astype(jnp.float32) * scale
        dist = jnp.abs(q_pos[:, None] - k_pos[None, :]).astype(jnp.float32)
        visible = (k_pos[None, :] // CHUNK) <= (q_pos[:, None] // CHUNK)
        bias = jnp.where(visible[None], -slopes[:, None, None] * dist[None], -jnp.inf)
        probs = jax.nn.softmax(scores + bias[None, :, None], axis=-1)
        weights = probs[:, :, 0] - lam * probs[:, :, 1]
        return jnp.einsum('bhqk,bhkv->bhqv', weights.astype(v.dtype), v_t)

    out = lax.map(attend_block, (q_blocks, jnp.arange(nq)))
    return out.transpose(1, 0, 3, 2, 4).reshape(b, s, nh, 2 * d)


def causal_dwconv(x, w, bias):
    y = lax.conv_general_dilated(
        x, w.astype(x.dtype), window_strides=(1,), padding=[(SSD_CONV - 1, 0)],
        dimension_numbers=('NWC', 'WIO', 'NWC'), feature_group_count=x.shape[-1])
    return y + bias.astype(x.dtype)


def ssd_chunked_scan(xdt, a, bm, cm):
    b, s, nh, p = xdt.shape
    g, n = bm.shape[2], bm.shape[3]
    r = nh // g
    nc = s // CHUNK
    X = xdt.reshape(b, nc, CHUNK, g, r, p)
    A = a.reshape(b, nc, CHUNK, g, r).transpose(0, 3, 4, 1, 2)
    Bc = bm.reshape(b, nc, CHUNK, g, n)
    Cc = cm.reshape(b, nc, CHUNK, g, n)
    a_cum = jnp.cumsum(A, axis=-1)
    tril = jnp.tril(jnp.ones((CHUNK, CHUNK), dtype=bool))
    seg = a_cum[..., :, None] - a_cum[..., None, :]
    L = jnp.exp(jnp.where(tril, seg, -jnp.inf))
    cb = jnp.einsum('bclgn,bcsgn->bcgls', Cc, Bc)
    y_diag = jnp.einsum('bcgls,bgrcls,bcsgrp->bclgrp', cb, L, X)
    decay_states = jnp.exp(a_cum[..., -1:] - a_cum)
    states = jnp.einsum('bcsgn,bgrcs,bcsgrp->cbgrpn', Bc, decay_states, X)
    chunk_decay = jnp.exp(a_cum[..., -1]).transpose(3, 0, 1, 2)

    def step(h, inp):
        st, dec = inp
        return h * dec[..., None, None] + st, h

    _, prev = lax.scan(step, jnp.zeros(states.shape[1:], states.dtype), (states, chunk_decay))
    y_off = jnp.einsum('bclgn,cbgrpn,bgrcl->bclgrp', Cc, prev, jnp.exp(a_cum))
    return (y_diag + y_off).reshape(b, s, nh, p)


def ssd_mixer(z, xbc, dt_raw, conv_w, conv_b, dt_bias, a_log, d_skip, g_ssd):
    b, s, _ = z.shape
    xbc = jax.nn.silu(causal_dwconv(xbc, conv_w, conv_b))
    xs, bm, cm = jnp.split(xbc, [SSD_WIDTH, SSD_WIDTH + SSD_GROUPS * SSD_STATE], axis=-1)
    xs = xs.reshape(b, s, SSD_HEADS, SSD_HEAD_DIM).astype(jnp.float32)
    bm = bm.reshape(b, s, SSD_GROUPS, SSD_STATE).astype(jnp.float32)
    cm = cm.reshape(b, s, SSD_GROUPS, SSD_STATE).astype(jnp.float32)
    dt = jax.nn.softplus(dt_raw.astype(jnp.float32) + dt_bias.astype(jnp.float32))
    a = -jnp.exp(a_log.astype(jnp.float32)) * dt
    y = ssd_chunked_scan(xs * dt[..., None], a, bm, cm) + d_skip.astype(jnp.float32)[:, None] * xs
    y = y.reshape(b, s, SSD_WIDTH).astype(z.dtype)
    return rms_norm(y * jax.nn.silu(z), g_ssd)


def peer(h, w_query, sub_keys, expert_u, expert_v):
    b, s, d = h.shape
    qry = (h @ w_query).reshape(b, s, PEER_HEADS, 2, PEER_HALF)
    sc = jnp.einsum('btnhd,nhkd->btnhk', qry, sub_keys)
    v1, i1 = lax.top_k(sc[..., 0, :], PEER_TOPK)
    v2, i2 = lax.top_k(sc[..., 1, :], PEER_TOPK)
    cand = (v1[..., :, None] + v2[..., None, :]).reshape(b, s, PEER_HEADS, PEER_TOPK * PEER_TOPK)
    cid = (i1[..., :, None] * PEER_NKEYS + i2[..., None, :]).reshape(b, s, PEER_HEADS, PEER_TOPK * PEER_TOPK)
    top, pos = lax.top_k(cand, PEER_TOPK)
    eid = jnp.take_along_axis(cid, pos, axis=-1)
    gate = jax.nn.softmax(top.astype(jnp.float32), axis=-1).astype(h.dtype)
    nblk = (b * s) // PEER_TOKEN_BLOCK
    hb = h.reshape(nblk, PEER_TOKEN_BLOCK, d)
    eb = eid.reshape(nblk, PEER_TOKEN_BLOCK, PEER_HEADS, PEER_TOPK)
    gb = gate.reshape(nblk, PEER_TOKEN_BLOCK, PEER_HEADS, PEER_TOPK)

    def expert_block(args):
        ht, et, gt = args
        u = expert_u[et]
        act = jax.nn.gelu(jnp.einsum('tnkd,td->tnk', u, ht), approximate=False) * gt
        return jnp.einsum('tnk,tnkd->td', act, expert_v[et])

    out = lax.map(expert_block, (hb, eb, gb))
    return out.reshape(b, s, d)


def setup_inputs(seed: int = 0) -> dict:
    key = jax.random.key(seed)
    ks = jax.random.split(key, 24)
    L = DEPTH

    def nrm(k, shape, scale):
        return jax.random.normal(k, shape, jnp.float32) * scale

    dt0 = jnp.exp(jax.random.uniform(ks[10], (L, SSD_HEADS), jnp.float32,
                                     minval=math.log(1e-3), maxval=math.log(1e-1)))
    return {
        "x": nrm(ks[0], (BATCH, SEQ, D_MODEL), 1.0),
        "g_mix": 1.0 + nrm(ks[1], (L, D_MODEL), 0.01),
        "w_in": nrm(ks[2], (L, D_MODEL, IN_WIDTH), D_MODEL ** -0.5),
        "lam_q1": nrm(ks[3], (L, DA_HEAD_DIM), 0.1),
        "lam_k1": nrm(ks[4], (L, DA_HEAD_DIM), 0.1),
        "lam_q2": nrm(ks[5], (L, DA_HEAD_DIM), 0.1),
        "lam_k2": nrm(ks[6], (L, DA_HEAD_DIM), 0.1),
        "g_subln": 1.0 + nrm(ks[7], (L, DA_V_DIM), 0.01),
        "conv_w": nrm(ks[8], (L, SSD_CONV, 1, SSD_XBC), SSD_CONV ** -0.5),
        "conv_b": nrm(ks[9], (L, SSD_XBC), 0.01),
        "dt_bias": dt0 + jnp.log(-jnp.expm1(-dt0)),
        "a_log": jnp.log(jax.random.uniform(ks[11], (L, SSD_HEADS), jnp.float32, minval=1.0, maxval=16.0)),
        "d_skip": 1.0 + nrm(ks[12], (L, SSD_HEADS), 0.1),
        "g_ssd": 1.0 + nrm(ks[13], (L, SSD_WIDTH), 0.01),
        "w_branch_a": nrm(ks[14], (L, DA_WIDTH, D_MODEL), DA_WIDTH ** -0.5),
        "w_branch_b": nrm(ks[15], (L, SSD_WIDTH, D_MODEL), SSD_WIDTH ** -0.5),
        "w_out": nrm(ks[16], (L, D_MODEL, D_MODEL), D_MODEL ** -0.5),
        "g_ffn": 1.0 + nrm(ks[17], (L, D_MODEL), 0.01),
        "w_query": nrm(ks[18], (L, D_MODEL, PEER_HEADS * PEER_KEY_DIM), D_MODEL ** -0.5),
        "sub_keys": nrm(ks[19], (L, PEER_HEADS, 2, PEER_NKEYS, PEER_HALF), PEER_HALF ** -0.5),
        "expert_u": nrm(ks[20], (L, PEER_EXPERTS, D_MODEL), D_MODEL ** -0.5),
        "expert_v": nrm(ks[21], (L, PEER_EXPERTS, D_MODEL), PEER_HEADS ** -0.5),
        "g_final": 1.0 + nrm(ks[22], (D_MODEL,), 0.01),
    }


def reference(x, g_mix, w_in, lam_q1, lam_k1, lam_q2, lam_k2, g_subln, conv_w, conv_b,
              dt_bias, a_log, d_skip, g_ssd, w_branch_a, w_branch_b, w_out, g_ffn,
              w_query, sub_keys, expert_u, expert_v, g_final):
    b, s, _ = x.shape
    for layer in range(DEPTH):
        lam0 = lambda_init(layer + 1)
        h = rms_norm(x, g_mix[layer])
        proj = h @ w_in[layer]
        q, k, v, z, xbc, dt_raw, gate_logits = jnp.split(proj, list(SPLITS), axis=-1)

        lam = (jnp.exp(jnp.sum(lam_q1[layer] * lam_k1[layer]).astype(jnp.float32))
               - jnp.exp(jnp.sum(lam_q2[layer] * lam_k2[layer]).astype(jnp.float32)) + lam0)
        att = diff_attention(q.reshape(b, s, DA_HEADS, 2, DA_HEAD_DIM),
                             k.reshape(b, s, DA_HEADS, 2, DA_HEAD_DIM),
                             v.reshape(b, s, DA_HEADS, DA_V_DIM), lam)
        att = rms_norm(att, g_subln[layer]) * (1.0 - lam0)
        y_a = att.reshape(b, s, DA_WIDTH) @ w_branch_a[layer]

        y_b = ssd_mixer(z, xbc, dt_raw, conv_w[layer], conv_b[layer], dt_bias[layer],
                        a_log[layer], d_skip[layer], g_ssd[layer]) @ w_branch_b[layer]

        g_a, g_b = jnp.split(jax.nn.sigmoid(gate_logits), N_BRANCH, axis=-1)
        x = x + (g_a * y_a + g_b * y_b) @ w_out[layer]

        x = x + peer(rms_norm(x, g_ffn[layer]), w_query[layer], sub_keys[layer],
                     expert_u[layer], expert_v[layer])
    return rms_norm(x, g_final)
```

```python
import functools
import math

import jax
import jax.numpy as jnp
from jax import lax
from jax.experimental import pallas as pl
from jax.experimental.pallas import tpu as pltpu

F32 = jnp.float32
BF16 = jnp.bfloat16

D_MODEL = 1024
CHUNK = 64
EPS = 1e-6
LOG2E = 1.4426950408889634

DA_HEADS = 8
DA_HEAD_DIM = 64
DA_V_DIM = 128

SSD_HEADS = 16
SSD_HEAD_DIM = 64
SSD_GROUPS = 4
SSD_STATE = 128
SSD_CONV = 4
SSD_XBC = 2048
GROUP_LANES = (SSD_HEADS // SSD_GROUPS) * SSD_HEAD_DIM

PEER_HEADS = 8
PEER_NKEYS = 128
PEER_EXPERTS = PEER_NKEYS * PEER_NKEYS
PEER_HALF = 128
PEER_TOPK = 16

N_MAIN = 8192
COL_Q, COL_K, COL_V, COL_Z, COL_XBC, COL_GATE = 0, 1024, 2048, 3072, 4096, 6144
LANES = 128

NEG_INF = float("-inf")


def _cparams(sem, vmem_mb):
    return pltpu.CompilerParams(dimension_semantics=sem, vmem_limit_bytes=vmem_mb * 1024 * 1024)


def _rms(x, g):
    return x * lax.rsqrt(jnp.mean(x * x, axis=-1, keepdims=True) + EPS) * g


def _dot(a, b):
    return jnp.dot(a, b, preferred_element_type=F32)


def _dot_nt(a, b):
    return lax.dot_general(a, b, (((1,), (1,)), ((), ())), preferred_element_type=F32)


def _dot_tn(a, b):
    return lax.dot_general(a, b, (((0,), (0,)), ((), ())), preferred_element_type=F32)


def _inproj_body(x_ref, g_ref, w_ref, wdt_ref, o_ref, dt_ref, h_scr):
    @pl.when(pl.program_id(1) == 0)
    def _():
        hb = _rms(x_ref[...], g_ref[...]).astype(BF16)
        h_scr[...] = hb
        dt_ref[...] = _dot(hb, wdt_ref[...])

    o_ref[...] = _dot(h_scr[...], w_ref[...]).astype(BF16)


def _inproj(x2, g_mix, w_main, w_dt, tm=1024, tn=1024):
    t = x2.shape[0]
    return pl.pallas_call(
        _inproj_body,
        grid=(t // tm, N_MAIN // tn),
        in_specs=[
            pl.BlockSpec((tm, D_MODEL), lambda i, j: (i, 0)),
            pl.BlockSpec((1, D_MODEL), lambda i, j: (0, 0)),
            pl.BlockSpec((D_MODEL, tn), lambda i, j: (0, j)),
            pl.BlockSpec((D_MODEL, LANES), lambda i, j: (0, 0)),
        ],
        out_specs=[
            pl.BlockSpec((tm, tn), lambda i, j: (i, j)),
            pl.BlockSpec((tm, LANES), lambda i, j: (i, 0)),
        ],
        out_shape=[
            jax.ShapeDtypeStruct((t, N_MAIN), BF16),
            jax.ShapeDtypeStruct((t, LANES), F32),
        ],
        scratch_shapes=[pltpu.VMEM((tm, D_MODEL), BF16)],
        compiler_params=_cparams(("arbitrary", "arbitrary"), 40),
        name="inproj",
    )(x2, g_mix, w_main, w_dt)


def _attn_body(lam0, tq, slope_ref, lq1_ref, lk1_ref, lq2_ref, lk2_ref, gsub_ref,
               q_ref, k_ref, v_ref, o_ref,
               boff, bdiag, acc1, acc2, m1, l1, m2, l2):
    qi = pl.program_id(2)
    slope2 = slope_ref[0][:, 0:1] * LOG2E

    @pl.when(qi == 0)
    def _():
        ii = lax.broadcasted_iota(jnp.int32, (tq, tq), 0)
        jj = lax.broadcasted_iota(jnp.int32, (tq, tq), 1)
        d = (ii - jj).astype(F32)
        boff[...] = -slope2 * d
        visible = (jj // CHUNK) <= (ii // CHUNK)
        bdiag[...] = jnp.where(visible, -slope2 * jnp.abs(d), NEG_INF)

    q = q_ref[0]
    lane = lax.broadcasted_iota(jnp.int32, q.shape, 1)
    zero = jnp.zeros_like(q)
    qa = jnp.where(lane < DA_HEAD_DIM, q, zero)
    qb = jnp.where(lane >= DA_HEAD_DIM, q, zero)

    for ref in (m1, m2):
        ref[...] = jnp.full(ref.shape, NEG_INF, F32)
    for ref in (l1, l2, acc1, acc2):
        ref[...] = jnp.zeros(ref.shape, F32)

    def block(kb, bias_ref, shift):
        k = k_ref[0, pl.ds(kb * tq, tq), :]
        v = v_ref[0, pl.ds(kb * tq, tq), :]
        for qm, acc, m, l in ((qa, acc1, m1, l1), (qb, acc2, m2, l2)):
            u = _dot_nt(qm, k) + bias_ref[...]
            m_old = m[...]
            m_new = jnp.maximum(m_old, jnp.max(u, axis=-1, keepdims=True) + shift)
            p = jnp.exp2(u - (m_new - shift))
            alpha = jnp.exp2(m_old - m_new)
            l[...] = alpha * l[...] + jnp.sum(p, axis=-1, keepdims=True)
            acc[...] = alpha * acc[...] + _dot(p.astype(BF16), v)
            m[...] = m_new

    block(qi, bdiag, jnp.zeros((1, 1), F32))

    def body(kb, carry):
        shift = -slope2 * ((qi - kb) * tq).astype(F32)
        block(kb, boff, shift)
        return carry

    lax.fori_loop(0, qi, body, 0)

    lam = (jnp.exp(jnp.sum(lq1_ref[...] * lk1_ref[...], axis=-1, keepdims=True))
           - jnp.exp(jnp.sum(lq2_ref[...] * lk2_ref[...], axis=-1, keepdims=True)) + lam0)
    o = acc1[...] / l1[...] - lam * (acc2[...] / l2[...])
    o_ref[0] = (_rms(o, gsub_ref[...]) * (1.0 - lam0)).astype(BF16)


def _attention(proj3, slopes, lq1, lk1, lq2, lk2, g_subln, lam0, tq=512):
    b, s, _ = proj3.shape
    kcol, vcol = COL_K // LANES, COL_V // LANES
    small = lambda shape: pl.BlockSpec(shape, lambda bi, h, qi: (0,) * len(shape))
    return pl.pallas_call(
        functools.partial(_attn_body, lam0, tq),
        grid=(b, DA_HEADS, s // tq),
        in_specs=[
            pl.BlockSpec((1, 1, LANES), lambda bi, h, qi: (h, 0, 0)),
            small((1, DA_HEAD_DIM)), small((1, DA_HEAD_DIM)),
            small((1, DA_HEAD_DIM)), small((1, DA_HEAD_DIM)),
            small((1, DA_V_DIM)),
            pl.BlockSpec((1, tq, LANES), lambda bi, h, qi: (bi, qi, h)),
            pl.BlockSpec((1, s, LANES), lambda bi, h, qi: (bi, 0, kcol + h)),
            pl.BlockSpec((1, s, LANES), lambda bi, h, qi: (bi, 0, vcol + h)),
        ],
        out_specs=pl.BlockSpec((1, tq, LANES), lambda bi, h, qi: (bi, qi, h)),
        out_shape=jax.ShapeDtypeStruct((b, s, DA_HEADS * DA_V_DIM), BF16),
        scratch_shapes=[
            pltpu.VMEM((tq, tq), F32), pltpu.VMEM((tq, tq), F32),
            pltpu.VMEM((tq, DA_V_DIM), F32), pltpu.VMEM((tq, DA_V_DIM), F32),
            pltpu.VMEM((tq, 1), F32), pltpu.VMEM((tq, 1), F32),
            pltpu.VMEM((tq, 1), F32), pltpu.VMEM((tq, 1), F32),
        ],
        compiler_params=_cparams(("arbitrary", "arbitrary", "arbitrary"), 40),
        name="diff_attention",
    )(slopes, lq1, lk1, lq2, lk2, g_subln, proj3, proj3, proj3)


def _split3(x):
    hi = x.astype(BF16)
    r = x - hi.astype(F32)
    mid = r.astype(BF16)
    lo = (r - mid.astype(F32)).astype(BF16)
    return hi, mid, lo


def _dot3_rhs(a_b, x):
    hi, mid, lo = _split3(x)
    return _dot(a_b, hi) + _dot(a_b, mid) + _dot(a_b, lo)


def _dot3_lhs(x, a_b):
    hi, mid, lo = _split3(x)
    return _dot(hi, a_b) + _dot(mid, a_b) + _dot(lo, a_b)


def _silu(x):
    return x * jax.nn.sigmoid(x)


def _ssd_body(ts, z_ref, xbc_ref, dt_ref, cw_ref, cb_ref, dtb_ref, alog_ref, dexp_ref, g_ref,
              tril_ref, ones_ref, expand_ref, eye_ref, trilrep_ref, bd_ref,
              o_ref, xe, state):
    ti = pl.program_id(1)

    @pl.when(ti == 0)
    def _():
        xe[0:8, :] = jnp.zeros((8, SSD_XBC), F32)
        state[...] = jnp.zeros(state.shape, F32)

    xe[8:8 + ts, :] = xbc_ref[0].astype(F32)
    conv = cb_ref[...] + cw_ref[0:1, :] * xe[5:5 + ts, :]
    for kk in range(1, SSD_CONV):
        conv = conv + cw_ref[kk:kk + 1, :] * xe[5 + kk:5 + kk + ts, :]
    tail = xe[ts:ts + 8, :]
    xe[0:8, :] = tail
    xc = _silu(conv)
    xs = xc[:, :D_MODEL]
    bm = xc[:, D_MODEL:D_MODEL + SSD_GROUPS * SSD_STATE].astype(BF16)
    cm = xc[:, D_MODEL + SSD_GROUPS * SSD_STATE:].astype(BF16)

    dt = jax.nn.softplus(dt_ref[0] + dtb_ref[...])
    a = -jnp.exp(alog_ref[...]) * dt
    acum = _dot3_rhs(tril_ref[...], a)
    colexp = _dot3_lhs(acum, expand_ref[...])
    dtexp = _dot3_lhs(dt, expand_ref[...])
    rowexp = _dot3_rhs(ones_ref[...], colexp * eye_ref[...])
    lmat = jnp.exp(jnp.where(trilrep_ref[...] > 0.5, colexp - rowexp, NEG_INF))
    xdt = xs * dtexp
    decay_in = jnp.exp(colexp)

    bdmask = bd_ref[...] > 0.5
    ys = []
    for c in range(ts // CHUNK):
        rows = slice(c * CHUNK, (c + 1) * CHUNK)
        a_last = colexp[(c + 1) * CHUNK - 1:(c + 1) * CHUNK, :]
        decay_st = jnp.exp(a_last - colexp[rows, :])
        chunk_decay = jnp.exp(a_last)
        xd = (xdt[rows, :] * decay_st).astype(BF16)
        xb = xdt[rows, :].astype(BF16)
        yg = []
        for g in range(SSD_GROUPS):
            lanes = slice(g * GROUP_LANES, (g + 1) * GROUP_LANES)
            cg = cm[rows, g * SSD_STATE:(g + 1) * SSD_STATE]
            bg = bm[rows, g * SSD_STATE:(g + 1) * SSD_STATE]
            cb = _dot_nt(cg, jnp.concatenate([bg] * 4, axis=0))
            w = (cb * lmat[rows, lanes]).astype(BF16)
            xg = xb[:, lanes]
            xbd = jnp.where(bdmask, jnp.concatenate([xg] * 4, axis=0), jnp.zeros((), BF16))
            y_diag = _dot(w, xbd)
            prev = state[g]
            y_off = _dot(cg, prev.astype(BF16)) * decay_in[rows, lanes]
            state[g] = prev * chunk_decay[:, lanes] + _dot_tn(bg, xd[:, lanes])
            yg.append(y_diag + y_off)
        ys.append(jnp.concatenate(yg, axis=1))
    y = jnp.concatenate(ys, axis=0) + dexp_ref[...] * xs
    y = y * _silu(z_ref[0].astype(F32))
    o_ref[0] = _rms(y, g_ref[...]).astype(BF16)


def _ssd_constants(ts):
    r = jnp.arange(ts)
    same_chunk = (r[:, None] // CHUNK) == (r[None, :] // CHUNK)
    tril = (same_chunk & (r[:, None] >= r[None, :])).astype(BF16)
    ones = same_chunk.astype(BF16)
    lane = jnp.arange(D_MODEL)
    expand = (jnp.arange(LANES)[:, None] == (lane[None, :] // SSD_HEAD_DIM)).astype(BF16)
    l_in = r[:, None] % CHUNK
    s_in = lane[None, :] % SSD_HEAD_DIM
    eye = (l_in == s_in).astype(F32)
    trilrep = (l_in >= s_in).astype(F32)
    q = jnp.arange(GROUP_LANES)
    bd = ((q[:, None] // SSD_HEAD_DIM) == (q[None, :] // SSD_HEAD_DIM)).astype(F32)
    return tril, ones, expand, eye, trilrep, bd


def _ssd(proj3, dt3, conv_w, conv_b, dt_bias, a_log, d_exp, g_ssd, ts=256):
    b, s, _ = proj3.shape
    consts = _ssd_constants(ts)
    full = lambda arr: pl.BlockSpec(arr.shape, lambda bi, ti: (0,) * arr.ndim)
    params = (conv_w, conv_b, dt_bias, a_log, d_exp, g_ssd)
    return pl.pallas_call(
        functools.partial(_ssd_body, ts),
        grid=(b, s // ts),
        in_specs=[
            pl.BlockSpec((1, ts, D_MODEL), lambda bi, ti: (bi, ti, COL_Z // D_MODEL)),
            pl.BlockSpec((1, ts, SSD_XBC), lambda bi, ti: (bi, ti, COL_XBC // SSD_XBC)),
            pl.BlockSpec((1, ts, LANES), lambda bi, ti: (bi, ti, 0)),
        ] + [full(p) for p in params] + [full(c) for c in consts],
        out_specs=pl.BlockSpec((1, ts, D_MODEL), lambda bi, ti: (bi, ti, 0)),
        out_shape=jax.ShapeDtypeStruct((b, s, D_MODEL), BF16),
        scratch_shapes=[
            pltpu.VMEM((ts + 8, SSD_XBC), F32),
            pltpu.VMEM((SSD_GROUPS, SSD_STATE, GROUP_LANES), F32),
        ],
        compiler_params=_cparams(("arbitrary", "arbitrary"), 48),
        name="ssd_mixer",
    )(proj3, proj3, dt3, *params, *consts)


def _merge_body(att_ref, ssd_ref, ga_ref, gb_ref, x_ref, wa_ref, wb_ref, wo_ref, o_ref):
    ya = _dot(att_ref[...], wa_ref[...])
    yb = _dot(ssd_ref[...], wb_ref[...])
    ga = jax.nn.sigmoid(ga_ref[...].astype(F32))
    gb = jax.nn.sigmoid(gb_ref[...].astype(F32))
    m = (ga * ya + gb * yb).astype(BF16)
    o_ref[...] = x_ref[...] + _dot(m, wo_ref[...])


def _merge(att2, ssd2, proj, x2, wa, wb, wo, tm=512):
    t = x2.shape[0]
    row = lambda c: pl.BlockSpec((tm, D_MODEL), lambda i: (i, c))
    wspec = pl.BlockSpec((D_MODEL, D_MODEL), lambda i: (0, 0))
    gcol = COL_GATE // D_MODEL
    return pl.pallas_call(
        _merge_body,
        grid=(t // tm,),
        in_specs=[row(0), row(0), row(gcol), row(gcol + 1), row(0), wspec, wspec, wspec],
        out_specs=row(0),
        out_shape=jax.ShapeDtypeStruct((t, D_MODEL), F32),
        compiler_params=_cparams(("arbitrary",), 40),
        name="gated_merge",
    )(att2, ssd2, proj, proj, x2, wa, wb, wo)


def _take_top(work, n):
    iota = lax.broadcasted_iota(jnp.int32, work.shape, 0)
    big = jnp.int32(work.shape[0])
    rows = []
    for _ in range(n):
        m = jnp.max(work, axis=0, keepdims=True)
        idx = jnp.min(jnp.where(work == m, iota, big), axis=0, keepdims=True)
        work = jnp.where(iota == idx, NEG_INF, work)
        rows.append(m)
    return rows, work


def _candidates(v1, v2, combine):
    half = PEER_TOPK // 2
    pieces = [combine(v1[0:1], v2)]
    for a in range(1, half):
        pieces.append(combine(v1[a:a + 1], v2[0:half]))
    pieces.append(combine(v1[half:], v2[0:1]))
    return jnp.concatenate(pieces, axis=0)


def _route_body(x_ref, g_ref, wq_ref, keys_ref, h_ref, a_ref, b_ref, tau_ref):
    hb = _rms(x_ref[...], g_ref[...]).astype(BF16)
    h_ref[...] = hb
    qt = _dot_nt(wq_ref[...], hb).astype(BF16)
    for n in range(PEER_HEADS):
        s1 = _dot(keys_ref[2 * n], qt[(2 * n) * PEER_HALF:(2 * n + 1) * PEER_HALF, :])
        s2 = _dot(keys_ref[2 * n + 1], qt[(2 * n + 1) * PEER_HALF:(2 * n + 2) * PEER_HALF, :])
        r1, _ = _take_top(s1, PEER_TOPK)
        r2, _ = _take_top(s2, PEER_TOPK)
        v1 = jnp.concatenate(r1, axis=0)
        v2 = jnp.concatenate(r2, axis=0)
        cand = _candidates(v1, v2, lambda p, q: p + q)
        _, masked = _take_top(cand, PEER_TOPK)
        sel = masked == NEG_INF
        top = cand[0:1]
        z = jnp.sum(jnp.where(sel, jnp.exp(cand - top), 0.0), axis=0, keepdims=True)
        inv_z = 1.0 / z
        a_ref[n] = jnp.exp(s1 - r1[0]) * inv_z
        b_ref[n] = jnp.exp(s2 - r2[0])
        pa = jnp.exp(v1 - r1[0]) * inv_z
        pb = jnp.exp(v2 - r2[0])
        prod = _candidates(pa, pb, lambda p, q: p * q)
        tau_ref[n] = jnp.min(jnp.where(sel, prod, jnp.inf), axis=0, keepdims=True)


def _route(x1, g_ffn, wq_t, keys, tt=256):
    t = x1.shape[0]
    return pl.pallas_call(
        _route_body,
        grid=(t // tt,),
        in_specs=[
            pl.BlockSpec((tt, D_MODEL), lambda i: (i, 0)),
            pl.BlockSpec((1, D_MODEL), lambda i: (0, 0)),
            pl.BlockSpec(wq_t.shape, lambda i: (0, 0)),
            pl.BlockSpec(keys.shape, lambda i: (0, 0, 0)),
        ],
        out_specs=[
            pl.BlockSpec((tt, D_MODEL), lambda i: (i, 0)),
            pl.BlockSpec((PEER_HEADS, PEER_NKEYS, tt), lambda i: (0, 0, i)),
            pl.BlockSpec((PEER_HEADS, PEER_NKEYS, tt), lambda i: (0, 0, i)),
            pl.BlockSpec((PEER_HEADS, 1, tt), lambda i: (0, 0, i)),
        ],
        out_shape=[
            jax.ShapeDtypeStruct((t, D_MODEL), BF16),
            jax.ShapeDtypeStruct((PEER_HEADS, PEER_NKEYS, t), F32),
            jax.ShapeDtypeStruct((PEER_HEADS, PEER_NKEYS, t), F32),
            jax.ShapeDtypeStruct((PEER_HEADS, 1, t), F32),
        ],
        compiler_params=_cparams(("arbitrary",), 48),
        name="peer_route",
    )(x1, g_ffn, wq_t, keys)


def _gelu(x):
    return 0.5 * x * (1.0 + lax.erf(x * (1.0 / math.sqrt(2.0))))


def _experts_body(te, h_ref, u_ref, vt_ref, a_ref, b_ref, tau_ref, x_ref, g_ref, o_ref, acc, act):
    e = pl.program_id(1)

    @pl.when(e == 0)
    def _():
        acc[...] = jnp.zeros(acc.shape, F32)

    st = _dot_nt(u_ref[...], h_ref[...])
    nblk = te // PEER_NKEYS
    for ib in range(nblk):
        rows = slice(ib * PEER_NKEYS, (ib + 1) * PEER_NKEYS)
        gate = None
        for n in range(PEER_HEADS):
            arow = a_ref[n, pl.ds(e * nblk + ib, 1), :]
            p = arow * b_ref[n]
            term = jnp.where(p >= tau_ref[n], p, 0.0)
            gate = term if gate is None else gate + term
        act[rows, :] = (gate * _gelu(st[rows, :])).astype(BF16)
    acc[...] += _dot(vt_ref[...], act[...])

    @pl.when(e == pl.num_programs(1) - 1)
    def _():
        o_ref[...] = _rms(x_ref[...] + acc[...].T, g_ref[...])


def _experts(h2, u_b, vt_b, a_f, b_f, tau, x1, g_final, tt=512, te=1024):
    t = x1.shape[0]
    return pl.pallas_call(
        functools.partial(_experts_body, te),
        grid=(t // tt, PEER_EXPERTS // te),
        in_specs=[
            pl.BlockSpec((tt, D_MODEL), lambda i, e: (i, 0)),
            pl.BlockSpec((te, D_MODEL), lambda i, e: (e, 0)),
            pl.BlockSpec((D_MODEL, te), lambda i, e: (0, e)),
            pl.BlockSpec((PEER_HEADS, PEER_NKEYS, tt), lambda i, e: (0, 0, i)),
            pl.BlockSpec((PEER_HEADS, PEER_NKEYS, tt), lambda i, e: (0, 0, i)),
            pl.BlockSpec((PEER_HEADS, 1, tt), lambda i, e: (0, 0, i)),
            pl.BlockSpec((tt, D_MODEL), lambda i, e: (i, 0)),
            pl.BlockSpec((1, D_MODEL), lambda i, e: (0, 0)),
        ],
        out_specs=pl.BlockSpec((tt, D_MODEL), lambda i, e: (i, 0)),
        out_shape=jax.ShapeDtypeStruct((t, D_MODEL), F32),
        scratch_shapes=[pltpu.VMEM((D_MODEL, tt), F32), pltpu.VMEM((te, tt), BF16)],
        compiler_params=_cparams(("arbitrary", "arbitrary"), 56),
        name="peer_experts",
    )(h2, u_b, vt_b, a_f, b_f, tau, x1, g_final)


def _lambda_init(layer_index):
    return 0.8 - 0.6 * math.exp(-0.3 * (layer_index - 1))


def _pad_lanes(v):
    return jnp.pad(v.reshape(1, -1), ((0, 0), (0, LANES - v.shape[-1])))


def _layer(x, layer, g_mix, w_in, lam_q1, lam_k1, lam_q2, lam_k2, g_subln, conv_w, conv_b, dt_bias,
           a_log, d_skip, g_ssd, w_branch_a, w_branch_b, w_out, g_ffn, w_query, sub_keys,
           expert_u, expert_v, g_out):
    b, s, d = x.shape
    t = b * s
    x2 = x.reshape(t, d)
    row = lambda v: v.reshape(1, -1)

    w = w_in[layer]
    dt_lo = COL_GATE
    q_scale = DA_HEAD_DIM ** -0.5 * LOG2E
    w_main = jnp.concatenate([w[:, :COL_K] * q_scale, w[:, COL_K:dt_lo], w[:, dt_lo + SSD_HEADS:]],
                             axis=1).astype(BF16)
    w_dt = jnp.pad(w[:, dt_lo:dt_lo + SSD_HEADS], ((0, 0), (0, LANES - SSD_HEADS))).astype(BF16)
    slopes = jnp.exp2(-8.0 * jnp.arange(1, DA_HEADS + 1, dtype=F32) / DA_HEADS)
    slopes = jnp.broadcast_to(slopes[:, None, None], (DA_HEADS, 1, LANES))

    proj, dt_raw = _inproj(x2, row(g_mix[layer]), w_main, w_dt)
    proj3 = proj.reshape(b, s, N_MAIN)

    att = _attention(proj3, slopes, row(lam_q1[layer]), row(lam_k1[layer]), row(lam_q2[layer]),
                     row(lam_k2[layer]), row(g_subln[layer]), _lambda_init(layer + 1))
    ssd = _ssd(proj3, dt_raw.reshape(b, s, LANES), conv_w[layer].reshape(SSD_CONV, SSD_XBC),
               row(conv_b[layer]), _pad_lanes(dt_bias[layer]), _pad_lanes(a_log[layer]),
               row(jnp.repeat(d_skip[layer], SSD_HEAD_DIM)), row(g_ssd[layer]))

    x1 = _merge(att.reshape(t, d), ssd.reshape(t, d), proj, x2, w_branch_a[layer].astype(BF16),
                w_branch_b[layer].astype(BF16), w_out[layer].astype(BF16))

    keys = sub_keys[layer].reshape(2 * PEER_HEADS, PEER_NKEYS, PEER_HALF).astype(BF16)
    h2, a_f, b_f, tau = _route(x1, row(g_ffn[layer]), w_query[layer].T.astype(BF16), keys)
    out = _experts(h2, expert_u[layer].astype(BF16), expert_v[layer].T.astype(BF16), a_f, b_f, tau,
                   x1, row(g_out))
    return out.reshape(b, s, d)


def kernel(x, g_mix, w_in, lam_q1, lam_k1, lam_q2, lam_k2, g_subln, conv_w, conv_b, dt_bias, a_log,
           d_skip, g_ssd, w_branch_a, w_branch_b, w_out, g_ffn, w_query, sub_keys, expert_u, expert_v,
           g_final):
    depth = g_mix.shape[0]
    assert depth == 1, "the final RMSNorm is fused into the last (only) layer's expert kernel"
    return _layer(x, 0, g_mix, w_in, lam_q1, lam_k1, lam_q2, lam_k2, g_subln, conv_w, conv_b, dt_bias,
                  a_log, d_skip, g_ssd, w_branch_a, w_branch_b, w_out, g_ffn, w_query, sub_keys,
                  expert_u, expert_v, g_final)
```

```python
import functools
import math

import jax
import jax.numpy as jnp
from jax import lax
from jax.experimental import pallas as pl
from jax.experimental.pallas import tpu as pltpu

F32 = jnp.float32
BF16 = jnp.bfloat16

D_MODEL = 1024
CHUNK = 64
EPS = 1e-6
LOG2E = 1.4426950408889634

DA_HEADS = 8
DA_HEAD_DIM = 64
DA_V_DIM = 128

SSD_HEADS = 16
SSD_HEAD_DIM = 64
SSD_GROUPS = 4
SSD_STATE = 128
SSD_CONV = 4
SSD_XBC = 2048
GROUP_LANES = (SSD_HEADS // SSD_GROUPS) * SSD_HEAD_DIM

PEER_HEADS = 8
PEER_NKEYS = 128
PEER_EXPERTS = PEER_NKEYS * PEER_NKEYS
PEER_HALF = 128
PEER_TOPK = 16

N_MAIN = 8192
COL_Q, COL_K, COL_V, COL_Z, COL_XBC, COL_GATE = 0, 1024, 2048, 3072, 4096, 6144
LANES = 128

NEG_INF = float("-inf")


def _cparams(sem, vmem_mb):
    return pltpu.CompilerParams(dimension_semantics=sem, vmem_limit_bytes=vmem_mb * 1024 * 1024)


def _rms(x, g):
    return x * lax.rsqrt(jnp.mean(x * x, axis=-1, keepdims=True) + EPS) * g


def _dot(a, b):
    return jnp.dot(a, b, preferred_element_type=F32)


def _dot_nt(a, b):
    return lax.dot_general(a, b, (((1,), (1,)), ((), ())), preferred_element_type=F32)


def _dot_tn(a, b):
    return lax.dot_general(a, b, (((0,), (0,)), ((), ())), preferred_element_type=F32)


def _inproj_body(x_ref, g_ref, w_ref, wdt_ref, o_ref, dt_ref, h_scr):
    @pl.when(pl.program_id(1) == 0)
    def _():
        hb = _rms(x_ref[...], g_ref[...]).astype(BF16)
        h_scr[...] = hb
        dt_ref[...] = _dot(hb, wdt_ref[...])

    o_ref[...] = _dot(h_scr[...], w_ref[...]).astype(BF16)


def _inproj(x2, g_mix, w_main, w_dt, tm=1024, tn=1024):
    t = x2.shape[0]
    return pl.pallas_call(
        _inproj_body,
        grid=(t // tm, N_MAIN // tn),
        in_specs=[
            pl.BlockSpec((tm, D_MODEL), lambda i, j: (i, 0)),
            pl.BlockSpec((1, D_MODEL), lambda i, j: (0, 0)),
            pl.BlockSpec((D_MODEL, tn), lambda i, j: (0, j)),
            pl.BlockSpec((D_MODEL, LANES), lambda i, j: (0, 0)),
        ],
        out_specs=[
            pl.BlockSpec((tm, tn), lambda i, j: (i, j)),
            pl.BlockSpec((tm, LANES), lambda i, j: (i, 0)),
        ],
        out_shape=[
            jax.ShapeDtypeStruct((t, N_MAIN), BF16),
            jax.ShapeDtypeStruct((t, LANES), F32),
        ],
        scratch_shapes=[pltpu.VMEM((tm, D_MODEL), BF16)],
        compiler_params=_cparams(("arbitrary", "arbitrary"), 40),
        name="inproj",
    )(x2, g_mix, w_main, w_dt)


def _attn_body(lam0, tq, slope_ref, lq1_ref, lk1_ref, lq2_ref, lk2_ref, gsub_ref,
               q_ref, k_ref, v_ref, o_ref,
               boff, bdiag, vt, acc1, acc2):
    qi = pl.program_id(2)
    slope2 = slope_ref[0][:, 0:1] * LOG2E
    n_kv = v_ref.shape[1] // tq

    @pl.when(qi == 0)
    def _():
        jj = lax.broadcasted_iota(jnp.int32, (tq, tq), 0)
        ii = lax.broadcasted_iota(jnp.int32, (tq, tq), 1)
        d = (ii - jj).astype(F32)
        boff[...] = -slope2 * d
        visible = (jj // CHUNK) <= (ii // CHUNK)
        bdiag[...] = jnp.where(visible, -slope2 * jnp.abs(d), NEG_INF)
        for c in range(n_kv):
            vt[:, c * tq:(c + 1) * tq] = v_ref[0, c * tq:(c + 1) * tq, :].astype(F32).T.astype(BF16)

    qt = q_ref[0].astype(F32).T.astype(BF16)
    row = lax.broadcasted_iota(jnp.int32, qt.shape, 0)
    zero = jnp.zeros_like(qt)
    qa = jnp.where(row < DA_HEAD_DIM, qt, zero)
    qb = jnp.where(row >= DA_HEAD_DIM, qt, zero)

    def block(kb, bias_ref, shift, stats, first):
        k = k_ref[0, pl.ds(pl.multiple_of(kb * tq, tq), tq), :]
        v_t = vt[:, pl.ds(pl.multiple_of(kb * tq, tq), tq)]
        out = []
        for qm, acc, (m_old, l_old) in zip((qa, qb), (acc1, acc2), stats):
            u = _dot(k, qm) + bias_ref[...]
            m_new = jnp.maximum(m_old, jnp.max(u, axis=0, keepdims=True) + shift)
            p = jnp.exp2(u - (m_new - shift))
            pv = _dot(v_t, p.astype(BF16))
            if first:
                l_new = jnp.sum(p, axis=0, keepdims=True)
                acc[...] = pv
            else:
                alpha = jnp.exp2(m_old - m_new)
                l_new = alpha * l_old + jnp.sum(p, axis=0, keepdims=True)
                acc[...] = alpha * acc[...] + pv
            out.append((m_new, l_new))
        return tuple(out)

    neg = jnp.full((1, tq), NEG_INF, F32)
    zeros = jnp.zeros((1, tq), F32)
    stats = block(qi, bdiag, jnp.zeros((1, 1), F32), ((neg, zeros), (neg, zeros)), True)

    def body(kb, stats):
        shift = -slope2 * ((qi - kb) * tq).astype(F32)
        return block(kb, boff, shift, stats, False)

    (_, l1), (_, l2) = lax.fori_loop(0, qi, body, stats)

    lam = (jnp.exp(jnp.sum(lq1_ref[...] * lk1_ref[...], axis=-1, keepdims=True))
           - jnp.exp(jnp.sum(lq2_ref[...] * lk2_ref[...], axis=-1, keepdims=True)) + lam0)
    o = acc1[...] / l1 - lam * (acc2[...] / l2)
    o = o * lax.rsqrt(jnp.mean(o * o, axis=0, keepdims=True) + EPS)
    o_ref[0] = (o.T * (gsub_ref[...] * (1.0 - lam0))).astype(BF16)


def _attention(proj3, slopes, lq1, lk1, lq2, lk2, g_subln, lam0, tq=512):
    b, s, _ = proj3.shape
    kcol, vcol = COL_K // LANES, COL_V // LANES
    small = lambda shape: pl.BlockSpec(shape, lambda bi, h, qi: (0,) * len(shape))
    return pl.pallas_call(
        functools.partial(_attn_body, lam0, tq),
        grid=(b, DA_HEADS, s // tq),
        in_specs=[
            pl.BlockSpec((1, 1, LANES), lambda bi, h, qi: (h, 0, 0)),
            small((1, DA_HEAD_DIM)), small((1, DA_HEAD_DIM)),
            small((1, DA_HEAD_DIM)), small((1, DA_HEAD_DIM)),
            small((1, DA_V_DIM)),
            pl.BlockSpec((1, tq, LANES), lambda bi, h, qi: (bi, qi, h)),
            pl.BlockSpec((1, s, LANES), lambda bi, h, qi: (bi, 0, kcol + h)),
            pl.BlockSpec((1, s, LANES), lambda bi, h, qi: (bi, 0, vcol + h)),
        ],
        out_specs=pl.BlockSpec((1, tq, LANES), lambda bi, h, qi: (bi, qi, h)),
        out_shape=jax.ShapeDtypeStruct((b, s, DA_HEADS * DA_V_DIM), BF16),
        scratch_shapes=[
            pltpu.VMEM((tq, tq), F32), pltpu.VMEM((tq, tq), F32),
            pltpu.VMEM((DA_V_DIM, s), BF16),
            pltpu.VMEM((DA_V_DIM, tq), F32), pltpu.VMEM((DA_V_DIM, tq), F32),
        ],
        compiler_params=_cparams(("arbitrary", "arbitrary", "arbitrary"), 40),
        name="diff_attention",
    )(slopes, lq1, lk1, lq2, lk2, g_subln, proj3, proj3, proj3)


def _split3(x):
    hi = x.astype(BF16)
    r = x - hi.astype(F32)
    mid = r.astype(BF16)
    lo = (r - mid.astype(F32)).astype(BF16)
    return hi, mid, lo


def _dot3_rhs(a_b, x):
    hi, mid, lo = _split3(x)
    return _dot(a_b, hi) + _dot(a_b, mid) + _dot(a_b, lo)


def _dot3_lhs(x, a_b):
    hi, mid, lo = _split3(x)
    return _dot(hi, a_b) + _dot(mid, a_b) + _dot(lo, a_b)


def _silu(x):
    return x * jax.nn.sigmoid(x)


def _ssd_body(ts, z_ref, xbc_ref, dt_ref, cw_ref, cb_ref, dtb_ref, alog_ref, dexp_ref, g_ref,
              tril_ref, ones_ref, expand_ref, eye_ref, trilrep_ref, bd_ref,
              o_ref, xe, state):
    ti = pl.program_id(1)

    @pl.when(ti == 0)
    def _():
        xe[0:8, :] = jnp.zeros((8, SSD_XBC), F32)
        state[...] = jnp.zeros(state.shape, F32)

    xe[8:8 + ts, :] = xbc_ref[0].astype(F32)
    conv = cb_ref[...] + cw_ref[0:1, :] * xe[5:5 + ts, :]
    for kk in range(1, SSD_CONV):
        conv = conv + cw_ref[kk:kk + 1, :] * xe[5 + kk:5 + kk + ts, :]
    tail = xe[ts:ts + 8, :]
    xe[0:8, :] = tail
    xc = _silu(conv)
    xs = xc[:, :D_MODEL]
    bm = xc[:, D_MODEL:D_MODEL + SSD_GROUPS * SSD_STATE].astype(BF16)
    cm = xc[:, D_MODEL + SSD_GROUPS * SSD_STATE:].astype(BF16)

    dt = jax.nn.softplus(dt_ref[0] + dtb_ref[...])
    a = -jnp.exp(alog_ref[...]) * dt
    acum = _dot3_rhs(tril_ref[...], a)
    colexp = _dot3_lhs(acum, expand_ref[...])
    dtexp = _dot3_lhs(dt, expand_ref[...])
    rowexp = _dot3_rhs(ones_ref[...], colexp * eye_ref[...])
    lmat = jnp.exp(jnp.where(trilrep_ref[...] > 0.5, colexp - rowexp, NEG_INF))
    xdt = xs * dtexp
    decay_in = jnp.exp(colexp)

    bdmask = bd_ref[...] > 0.5
    ys = []
    for c in range(ts // CHUNK):
        rows = slice(c * CHUNK, (c + 1) * CHUNK)
        a_last = colexp[(c + 1) * CHUNK - 1:(c + 1) * CHUNK, :]
        decay_st = jnp.exp(a_last - colexp[rows, :])
        chunk_decay = jnp.exp(a_last)
        xd = (xdt[rows, :] * decay_st).astype(BF16)
        xb = xdt[rows, :].astype(BF16)
        yg = []
        for g in range(SSD_GROUPS):
            lanes = slice(g * GROUP_LANES, (g + 1) * GROUP_LANES)
            cg = cm[rows, g * SSD_STATE:(g + 1) * SSD_STATE]
            bg = bm[rows, g * SSD_STATE:(g + 1) * SSD_STATE]
            cb = _dot_nt(cg, jnp.concatenate([bg] * 4, axis=0))
            w = (cb * lmat[rows, lanes]).astype(BF16)
            xg = xb[:, lanes]
            xbd = jnp.where(bdmask, jnp.concatenate([xg] * 4, axis=0), jnp.zeros((), BF16))
            y_diag = _dot(w, xbd)
            prev = state[g]
            y_off = _dot(cg, prev.astype(BF16)) * decay_in[rows, lanes]
            state[g] = prev * chunk_decay[:, lanes] + _dot_tn(bg, xd[:, lanes])
            yg.append(y_diag + y_off)
        ys.append(jnp.concatenate(yg, axis=1))
    y = jnp.concatenate(ys, axis=0) + dexp_ref[...] * xs
    y = y * _silu(z_ref[0].astype(F32))
    o_ref[0] = _rms(y, g_ref[...]).astype(BF16)


def _ssd_constants(ts):
    r = jnp.arange(ts)
    same_chunk = (r[:, None] // CHUNK) == (r[None, :] // CHUNK)
    tril = (same_chunk & (r[:, None] >= r[None, :])).astype(BF16)
    ones = same_chunk.astype(BF16)
    lane = jnp.arange(D_MODEL)
    expand = (jnp.arange(LANES)[:, None] == (lane[None, :] // SSD_HEAD_DIM)).astype(BF16)
    l_in = r[:, None] % CHUNK
    s_in = lane[None, :] % SSD_HEAD_DIM
    eye = (l_in == s_in).astype(F32)
    trilrep = (l_in >= s_in).astype(F32)
    q = jnp.arange(GROUP_LANES)
    bd = ((q[:, None] // SSD_HEAD_DIM) == (q[None, :] // SSD_HEAD_DIM)).astype(F32)
    return tril, ones, expand, eye, trilrep, bd


def _ssd(proj3, dt3, conv_w, conv_b, dt_bias, a_log, d_exp, g_ssd, ts=256):
    b, s, _ = proj3.shape
    consts = _ssd_constants(ts)
    full = lambda arr: pl.BlockSpec(arr.shape, lambda bi, ti: (0,) * arr.ndim)
    params = (conv_w, conv_b, dt_bias, a_log, d_exp, g_ssd)
    return pl.pallas_call(
        functools.partial(_ssd_body, ts),
        grid=(b, s // ts),
        in_specs=[
            pl.BlockSpec((1, ts, D_MODEL), lambda bi, ti: (bi, ti, COL_Z // D_MODEL)),
            pl.BlockSpec((1, ts, SSD_XBC), lambda bi, ti: (bi, ti, COL_XBC // SSD_XBC)),
            pl.BlockSpec((1, ts, LANES), lambda bi, ti: (bi, ti, 0)),
        ] + [full(p) for p in params] + [full(c) for c in consts],
        out_specs=pl.BlockSpec((1, ts, D_MODEL), lambda bi, ti: (bi, ti, 0)),
        out_shape=jax.ShapeDtypeStruct((b, s, D_MODEL), BF16),
        scratch_shapes=[
            pltpu.VMEM((ts + 8, SSD_XBC), F32),
            pltpu.VMEM((SSD_GROUPS, SSD_STATE, GROUP_LANES), F32),
        ],
        compiler_params=_cparams(("arbitrary", "arbitrary"), 48),
        name="ssd_mixer",
    )(proj3, proj3, dt3, *params, *consts)


def _merge_body(att_ref, ssd_ref, ga_ref, gb_ref, x_ref, wa_ref, wb_ref, wo_ref, o_ref):
    ya = _dot(att_ref[...], wa_ref[...])
    yb = _dot(ssd_ref[...], wb_ref[...])
    ga = jax.nn.sigmoid(ga_ref[...].astype(F32))
    gb = jax.nn.sigmoid(gb_ref[...].astype(F32))
    m = (ga * ya + gb * yb).astype(BF16)
    o_ref[...] = x_ref[...] + _dot(m, wo_ref[...])


def _merge(att2, ssd2, proj, x2, wa, wb, wo, tm=512):
    t = x2.shape[0]
    row = lambda c: pl.BlockSpec((tm, D_MODEL), lambda i: (i, c))
    wspec = pl.BlockSpec((D_MODEL, D_MODEL), lambda i: (0, 0))
    gcol = COL_GATE // D_MODEL
    return pl.pallas_call(
        _merge_body,
        grid=(t // tm,),
        in_specs=[row(0), row(0), row(gcol), row(gcol + 1), row(0), wspec, wspec, wspec],
        out_specs=row(0),
        out_shape=jax.ShapeDtypeStruct((t, D_MODEL), F32),
        compiler_params=_cparams(("arbitrary",), 40),
        name="gated_merge",
    )(att2, ssd2, proj, proj, x2, wa, wb, wo)


def _take_top(work, n):
    iota = lax.broadcasted_iota(jnp.int32, work.shape, 0)
    big = jnp.int32(work.shape[0])
    rows = []
    for _ in range(n):
        m = jnp.max(work, axis=0, keepdims=True)
        idx = jnp.min(jnp.where(work == m, iota, big), axis=0, keepdims=True)
        work = jnp.where(iota == idx, NEG_INF, work)
        rows.append(m)
    return rows, work


def _candidates(v1, v2, combine):
    half = PEER_TOPK // 2
    pieces = [combine(v1[0:1], v2)]
    for a in range(1, half):
        pieces.append(combine(v1[a:a + 1], v2[0:half]))
    pieces.append(combine(v1[half:], v2[0:1]))
    return jnp.concatenate(pieces, axis=0)


def _route_body(x_ref, g_ref, wq_ref, keys_ref, h_ref, a_ref, b_ref, tau_ref):
    hb = _rms(x_ref[...], g_ref[...]).astype(BF16)
    h_ref[...] = hb
    qt = _dot_nt(wq_ref[...], hb).astype(BF16)
    for n in range(PEER_HEADS):
        s1 = _dot(keys_ref[2 * n], qt[(2 * n) * PEER_HALF:(2 * n + 1) * PEER_HALF, :])
        s2 = _dot(keys_ref[2 * n + 1], qt[(2 * n + 1) * PEER_HALF:(2 * n + 2) * PEER_HALF, :])
        r1, _ = _take_top(s1, PEER_TOPK)
        r2, _ = _take_top(s2, PEER_TOPK)
        v1 = jnp.concatenate(r1, axis=0)
        v2 = jnp.concatenate(r2, axis=0)
        cand = _candidates(v1, v2, lambda p, q: p + q)
        _, masked = _take_top(cand, PEER_TOPK)
        sel = masked == NEG_INF
        top = cand[0:1]
        z = jnp.sum(jnp.where(sel, jnp.exp(cand - top), 0.0), axis=0, keepdims=True)
        inv_z = 1.0 / z
        a_ref[n] = jnp.exp(s1 - r1[0]) * inv_z
        b_ref[n] = jnp.exp(s2 - r2[0])
        pa = jnp.exp(v1 - r1[0]) * inv_z
        pb = jnp.exp(v2 - r2[0])
        prod = _candidates(pa, pb, lambda p, q: p * q)
        tau_ref[n] = jnp.min(jnp.where(sel, prod, jnp.inf), axis=0, keepdims=True)


def _route(x1, g_ffn, wq_t, keys, tt=256):
    t = x1.shape[0]
    return pl.pallas_call(
        _route_body,
        grid=(t // tt,),
        in_specs=[
            pl.BlockSpec((tt, D_MODEL), lambda i: (i, 0)),
            pl.BlockSpec((1, D_MODEL), lambda i: (0, 0)),
            pl.BlockSpec(wq_t.shape, lambda i: (0, 0)),
            pl.BlockSpec(keys.shape, lambda i: (0, 0, 0)),
        ],
        out_specs=[
            pl.BlockSpec((tt, D_MODEL), lambda i: (i, 0)),
            pl.BlockSpec((PEER_HEADS, PEER_NKEYS, tt), lambda i: (0, 0, i)),
            pl.BlockSpec((PEER_HEADS, PEER_NKEYS, tt), lambda i: (0, 0, i)),
            pl.BlockSpec((PEER_HEADS, 1, tt), lambda i: (0, 0, i)),
        ],
        out_shape=[
            jax.ShapeDtypeStruct((t, D_MODEL), BF16),
            jax.ShapeDtypeStruct((PEER_HEADS, PEER_NKEYS, t), F32),
            jax.ShapeDtypeStruct((PEER_HEADS, PEER_NKEYS, t), F32),
            jax.ShapeDtypeStruct((PEER_HEADS, 1, t), F32),
        ],
        compiler_params=_cparams(("arbitrary",), 48),
        name="peer_route",
    )(x1, g_ffn, wq_t, keys)


def _gelu(x):
    return 0.5 * x * (1.0 + lax.erf(x * (1.0 / math.sqrt(2.0))))


def _experts_body(te, h_ref, u_ref, vt_ref, a_ref, b_ref, tau_ref, x_ref, g_ref, o_ref, acc, act):
    e = pl.program_id(1)

    @pl.when(e == 0)
    def _():
        acc[...] = jnp.zeros(acc.shape, F32)

    st = _dot_nt(u_ref[...], h_ref[...])
    nblk = te // PEER_NKEYS
    for ib in range(nblk):
        rows = slice(ib * PEER_NKEYS, (ib + 1) * PEER_NKEYS)
        gate = None
        for n in range(PEER_HEADS):
            arow = a_ref[n, pl.ds(e * nblk + ib, 1), :]
            p = arow * b_ref[n]
            term = jnp.where(p >= tau_ref[n], p, 0.0)
            gate = term if gate is None else gate + term
        act[rows, :] = (gate * _gelu(st[rows, :])).astype(BF16)
    acc[...] += _dot(vt_ref[...], act[...])

    @pl.when(e == pl.num_programs(1) - 1)
    def _():
        o_ref[...] = _rms(x_ref[...] + acc[...].T, g_ref[...])


def _experts(h2, u_b, vt_b, a_f, b_f, tau, x1, g_final, tt=512, te=1024):
    t = x1.shape[0]
    return pl.pallas_call(
        functools.partial(_experts_body, te),
        grid=(t // tt, PEER_EXPERTS // te),
        in_specs=[
            pl.BlockSpec((tt, D_MODEL), lambda i, e: (i, 0)),
            pl.BlockSpec((te, D_MODEL), lambda i, e: (e, 0)),
            pl.BlockSpec((D_MODEL, te), lambda i, e: (0, e)),
            pl.BlockSpec((PEER_HEADS, PEER_NKEYS, tt), lambda i, e: (0, 0, i)),
            pl.BlockSpec((PEER_HEADS, PEER_NKEYS, tt), lambda i, e: (0, 0, i)),
            pl.BlockSpec((PEER_HEADS, 1, tt), lambda i, e: (0, 0, i)),
            pl.BlockSpec((tt, D_MODEL), lambda i, e: (i, 0)),
            pl.BlockSpec((1, D_MODEL), lambda i, e: (0, 0)),
        ],
        out_specs=pl.BlockSpec((tt, D_MODEL), lambda i, e: (i, 0)),
        out_shape=jax.ShapeDtypeStruct((t, D_MODEL), F32),
        scratch_shapes=[pltpu.VMEM((D_MODEL, tt), F32), pltpu.VMEM((te, tt), BF16)],
        compiler_params=_cparams(("arbitrary", "arbitrary"), 56),
        name="peer_experts",
    )(h2, u_b, vt_b, a_f, b_f, tau, x1, g_final)


def _lambda_init(layer_index):
    return 0.8 - 0.6 * math.exp(-0.3 * (layer_index - 1))


def _pad_lanes(v):
    return jnp.pad(v.reshape(1, -1), ((0, 0), (0, LANES - v.shape[-1])))


def _layer(x, layer, g_mix, w_in, lam_q1, lam_k1, lam_q2, lam_k2, g_subln, conv_w, conv_b, dt_bias,
           a_log, d_skip, g_ssd, w_branch_a, w_branch_b, w_out, g_ffn, w_query, sub_keys,
           expert_u, expert_v, g_out):
    b, s, d = x.shape
    t = b * s
    x2 = x.reshape(t, d)
    row = lambda v: v.reshape(1, -1)

    w = w_in[layer]
    dt_lo = COL_GATE
    q_scale = DA_HEAD_DIM ** -0.5 * LOG2E
    w_main = jnp.concatenate([w[:, :COL_K] * q_scale, w[:, COL_K:dt_lo], w[:, dt_lo + SSD_HEADS:]],
                             axis=1).astype(BF16)
    w_dt = jnp.pad(w[:, dt_lo:dt_lo + SSD_HEADS], ((0, 0), (0, LANES - SSD_HEADS))).astype(BF16)
    slopes = jnp.exp2(-8.0 * jnp.arange(1, DA_HEADS + 1, dtype=F32) / DA_HEADS)
    slopes = jnp.broadcast_to(slopes[:, None, None], (DA_HEADS, 1, LANES))

    proj, dt_raw = _inproj(x2, row(g_mix[layer]), w_main, w_dt)
    proj3 = proj.reshape(b, s, N_MAIN)

    att = _attention(proj3, slopes, row(lam_q1[layer]), row(lam_k1[layer]), row(lam_q2[layer]),
                     row(lam_k2[layer]), row(g_subln[layer]), _lambda_init(layer + 1))
    ssd = _ssd(proj3, dt_raw.reshape(b, s, LANES), conv_w[layer].reshape(SSD_CONV, SSD_XBC),
               row(conv_b[layer]), _pad_lanes(dt_bias[layer]), _pad_lanes(a_log[layer]),
               row(jnp.repeat(d_skip[layer], SSD_HEAD_DIM)), row(g_ssd[layer]))

    x1 = _merge(att.reshape(t, d), ssd.reshape(t, d), proj, x2, w_branch_a[layer].astype(BF16),
                w_branch_b[layer].astype(BF16), w_out[layer].astype(BF16))

    keys = sub_keys[layer].reshape(2 * PEER_HEADS, PEER_NKEYS, PEER_HALF).astype(BF16)
    h2, a_f, b_f, tau = _route(x1, row(g_ffn[layer]), w_query[layer].T.astype(BF16), keys)
    out = _experts(h2, expert_u[layer].astype(BF16), expert_v[layer].T.astype(BF16), a_f, b_f, tau,
                   x1, row(g_out))
    return out.reshape(b, s, d)


def kernel(x, g_mix, w_in, lam_q1, lam_k1, lam_q2, lam_k2, g_subln, conv_w, conv_b, dt_bias, a_log,
           d_skip, g_ssd, w_branch_a, w_branch_b, w_out, g_ffn, w_query, sub_keys, expert_u, expert_v,
           g_final):
    depth = g_mix.shape[0]
    assert depth == 1, "the final RMSNorm is fused into the last (only) layer's expert kernel"
    return _layer(x, 0, g_mix, w_in, lam_q1, lam_k1, lam_q2, lam_k2, g_subln, conv_w, conv_b, dt_bias,
                  a_log, d_skip, g_ssd, w_branch_a, w_branch_b, w_out, g_ffn, w_query, sub_keys,
                  expert_u, expert_v, g_final)
```

```python
import functools
import math

import jax
import jax.numpy as jnp
from jax import lax
from jax.experimental import pallas as pl
from jax.experimental.pallas import tpu as pltpu

F32 = jnp.float32
BF16 = jnp.bfloat16

D_MODEL = 1024
CHUNK = 64
EPS = 1e-6
LOG2E = 1.4426950408889634

DA_HEADS = 8
DA_HEAD_DIM = 64
DA_V_DIM = 128

SSD_HEADS = 16
SSD_HEAD_DIM = 64
SSD_GROUPS = 4
SSD_STATE = 128
SSD_CONV = 4
SSD_XBC = 2048
GROUP_LANES = (SSD_HEADS // SSD_GROUPS) * SSD_HEAD_DIM

PEER_HEADS = 8
PEER_NKEYS = 128
PEER_EXPERTS = PEER_NKEYS * PEER_NKEYS
PEER_HALF = 128
PEER_TOPK = 16

N_MAIN = 8192
COL_Q, COL_K, COL_V, COL_Z, COL_XBC, COL_GATE = 0, 1024, 2048, 3072, 4096, 6144
LANES = 128

NEG_INF = float("-inf")


def _cparams(sem, vmem_mb):
    return pltpu.CompilerParams(dimension_semantics=sem, vmem_limit_bytes=vmem_mb * 1024 * 1024)


def _rms(x, g):
    return x * lax.rsqrt(jnp.mean(x * x, axis=-1, keepdims=True) + EPS) * g


def _dot(a, b):
    return jnp.dot(a, b, preferred_element_type=F32)


def _dot_nt(a, b):
    return lax.dot_general(a, b, (((1,), (1,)), ((), ())), preferred_element_type=F32)


def _dot_tn(a, b):
    return lax.dot_general(a, b, (((0,), (0,)), ((), ())), preferred_element_type=F32)


def _inproj_body(x_ref, g_ref, w_ref, wdt_ref, o_ref, dt_ref, h_scr):
    @pl.when(pl.program_id(1) == 0)
    def _():
        hb = _rms(x_ref[...], g_ref[...]).astype(BF16)
        h_scr[...] = hb
        dt_ref[...] = _dot(hb, wdt_ref[...])

    o_ref[...] = _dot(h_scr[...], w_ref[...]).astype(BF16)


def _inproj(x2, g_mix, w_main, w_dt, tm=1024, tn=1024):
    t = x2.shape[0]
    return pl.pallas_call(
        _inproj_body,
        grid=(t // tm, N_MAIN // tn),
        in_specs=[
            pl.BlockSpec((tm, D_MODEL), lambda i, j: (i, 0)),
            pl.BlockSpec((1, D_MODEL), lambda i, j: (0, 0)),
            pl.BlockSpec((D_MODEL, tn), lambda i, j: (0, j)),
            pl.BlockSpec((D_MODEL, LANES), lambda i, j: (0, 0)),
        ],
        out_specs=[
            pl.BlockSpec((tm, tn), lambda i, j: (i, j)),
            pl.BlockSpec((tm, LANES), lambda i, j: (i, 0)),
        ],
        out_shape=[
            jax.ShapeDtypeStruct((t, N_MAIN), BF16),
            jax.ShapeDtypeStruct((t, LANES), F32),
        ],
        scratch_shapes=[pltpu.VMEM((tm, D_MODEL), BF16)],
        compiler_params=_cparams(("arbitrary", "arbitrary"), 40),
        name="inproj",
    )(x2, g_mix, w_main, w_dt)


ATT_ROWS = DA_V_DIM + 16


def _attn_body(lam0, tq, slope_ref, lq1_ref, lk1_ref, lq2_ref, lk2_ref, gsub_ref,
               q_ref, k_ref, v_ref, o_ref,
               bdiag, ka, kb_, vt, acc1, acc2, ua, ub):
    qi = pl.program_id(2)
    slope2 = slope_ref[0][:, 0:1] * LOG2E
    n_kv = v_ref.shape[1] // tq

    @pl.when(qi == 0)
    def _():
        jj = lax.broadcasted_iota(jnp.int32, (tq, tq), 0)
        ii = lax.broadcasted_iota(jnp.int32, (tq, tq), 1)
        d = (ii - jj).astype(F32)
        visible = (jj // CHUNK) <= (ii // CHUNK)
        bdiag[...] = jnp.where(visible, -slope2 * (jnp.abs(d) - d), NEG_INF)

        lane = lax.broadcasted_iota(jnp.int32, (tq, LANES), 1)
        c = slope2 * lax.broadcasted_iota(jnp.int32, (tq, LANES), 0).astype(F32)
        c_hi = c.astype(BF16).astype(F32)
        c_mid = (c - c_hi).astype(BF16).astype(F32)
        c_lo = c - c_hi - c_mid

        def extras(base):
            return jnp.where(lane == base, c_hi, jnp.where(lane == base + 1, c_mid,
                             jnp.where(lane == base + 2, c_lo, 0.0)))

        ext_a, ext_b = extras(DA_HEAD_DIM), extras(0)
        ones_row = lax.broadcasted_iota(jnp.int32, (ATT_ROWS - DA_V_DIM, tq), 0) == 0
        for cblk in range(n_kv):
            rows = slice(cblk * tq, (cblk + 1) * tq)
            kblk = k_ref[0, rows, :].astype(F32)
            ka[rows, :] = jnp.where(lane < DA_HEAD_DIM, kblk, ext_a).astype(BF16)
            kb_[rows, :] = jnp.where(lane >= DA_HEAD_DIM, kblk, ext_b).astype(BF16)
            vt[0:DA_V_DIM, rows] = v_ref[0, rows, :].astype(F32).T.astype(BF16)
            vt[DA_V_DIM:, rows] = jnp.where(ones_row, 1.0, 0.0).astype(BF16)

    qt = q_ref[0].astype(F32).T
    row = lax.broadcasted_iota(jnp.int32, qt.shape, 0)
    qa = jnp.where(row < DA_HEAD_DIM, qt, jnp.where(row < DA_HEAD_DIM + 3, 1.0, 0.0)).astype(BF16)
    qb = jnp.where(row >= DA_HEAD_DIM, qt, jnp.where(row < 3, 1.0, 0.0)).astype(BF16)

    streams = ((ka, qa, acc1), (kb_, qb, acc2))

    def scores(start, rows, dst):
        for mi, (kref, qm, _) in enumerate(streams):
            dst[mi] = _dot(kref[pl.ds(start, rows), :], qm)

    def absorb(src, start, bias, shift, ms):
        v_t = vt[:, pl.ds(start, hk)]
        out = []
        for mi, ((_, _, acc), m_old) in enumerate(zip(streams, ms)):
            u = src[mi]
            if bias is not None:
                u = u + bias
            m_new = jnp.maximum(m_old, jnp.max(u, axis=0, keepdims=True) + shift)
            p = jnp.exp2(u - (m_new - shift))
            pv = _dot(v_t, p.astype(BF16))
            acc[...] = jnp.exp2(m_old - m_new) * acc[...] + pv
            out.append(m_new)
        return tuple(out)

    hk = tq // 2
    acc1[...] = jnp.zeros(acc1.shape, F32)
    acc2[...] = jnp.zeros(acc2.shape, F32)
    neg = jnp.full((1, tq), NEG_INF, F32)
    scores(0, hk, ua)

    def body(j, ms):
        s0 = pl.multiple_of(2 * j * hk, hk)
        shift = -slope2 * ((qi - j) * tq).astype(F32)
        scores(s0 + hk, hk, ub)
        ms = absorb(ua, s0, None, shift, ms)
        scores(s0 + 2 * hk, hk, ua)
        return absorb(ub, s0 + hk, None, shift, ms)

    ms = lax.fori_loop(0, qi, body, (neg, neg))
    diag = pl.multiple_of(qi * tq, tq)
    no_shift = jnp.zeros((1, 1), F32)
    scores(diag + hk, hk, ub)
    ms = absorb(ua, diag, bdiag[0:hk, :], no_shift, ms)
    absorb(ub, diag + hk, bdiag[hk:tq, :], no_shift, ms)

    lam = (jnp.exp(jnp.sum(lq1_ref[...] * lk1_ref[...], axis=-1, keepdims=True))
           - jnp.exp(jnp.sum(lq2_ref[...] * lk2_ref[...], axis=-1, keepdims=True)) + lam0)
    o = (acc1[0:DA_V_DIM, :] / acc1[DA_V_DIM:DA_V_DIM + 1, :]
         - lam * (acc2[0:DA_V_DIM, :] / acc2[DA_V_DIM:DA_V_DIM + 1, :]))
    o = o * lax.rsqrt(jnp.mean(o * o, axis=0, keepdims=True) + EPS)
    o_ref[0] = (o.T * (gsub_ref[...] * (1.0 - lam0))).astype(BF16)


def _attention(proj3, slopes, lq1, lk1, lq2, lk2, g_subln, lam0, tq=512):
    b, s, _ = proj3.shape
    kcol, vcol = COL_K // LANES, COL_V // LANES
    small = lambda shape: pl.BlockSpec(shape, lambda bi, h, qi: (0,) * len(shape))
    return pl.pallas_call(
        functools.partial(_attn_body, lam0, tq),
        grid=(b, DA_HEADS, s // tq),
        in_specs=[
            pl.BlockSpec((1, 1, LANES), lambda bi, h, qi: (h, 0, 0)),
            small((1, DA_HEAD_DIM)), small((1, DA_HEAD_DIM)),
            small((1, DA_HEAD_DIM)), small((1, DA_HEAD_DIM)),
            small((1, DA_V_DIM)),
            pl.BlockSpec((1, tq, LANES), lambda bi, h, qi: (bi, qi, h)),
            pl.BlockSpec((1, s, LANES), lambda bi, h, qi: (bi, 0, kcol + h)),
            pl.BlockSpec((1, s, LANES), lambda bi, h, qi: (bi, 0, vcol + h)),
        ],
        out_specs=pl.BlockSpec((1, tq, LANES), lambda bi, h, qi: (bi, qi, h)),
        out_shape=jax.ShapeDtypeStruct((b, s, DA_HEADS * DA_V_DIM), BF16),
        scratch_shapes=[
            pltpu.VMEM((tq, tq), F32),
            pltpu.VMEM((s, LANES), BF16), pltpu.VMEM((s, LANES), BF16),
            pltpu.VMEM((ATT_ROWS, s), BF16),
            pltpu.VMEM((ATT_ROWS, tq), F32), pltpu.VMEM((ATT_ROWS, tq), F32),
            pltpu.VMEM((2, tq // 2, tq), F32), pltpu.VMEM((2, tq // 2, tq), F32),
        ],
        compiler_params=_cparams(("arbitrary", "arbitrary", "arbitrary"), 40),
        name="diff_attention",
    )(slopes, lq1, lk1, lq2, lk2, g_subln, proj3, proj3, proj3)


def _split3(x):
    hi = x.astype(BF16)
    r = x - hi.astype(F32)
    mid = r.astype(BF16)
    lo = (r - mid.astype(F32)).astype(BF16)
    return hi, mid, lo


def _dot3_rhs(a_b, x):
    hi, mid, lo = _split3(x)
    return _dot(a_b, hi) + _dot(a_b, mid) + _dot(a_b, lo)


def _dot3_lhs(x, a_b):
    hi, mid, lo = _split3(x)
    return _dot(hi, a_b) + _dot(mid, a_b) + _dot(lo, a_b)


def _silu(x):
    return x * jax.nn.sigmoid(x)


def _ssd_body(ts, z_ref, xbc_ref, dt_ref, cw_ref, cb_ref, dtb_ref, alog_ref, dexp_ref, g_ref,
              tril_ref, ones_ref, expand_ref, eye_ref, trilrep_ref, bd_ref,
              o_ref, xe, state):
    ti = pl.program_id(1)

    @pl.when(ti == 0)
    def _():
        xe[0:8, :] = jnp.zeros((8, SSD_XBC), F32)
        state[...] = jnp.zeros(state.shape, F32)

    xe[8:8 + ts, :] = xbc_ref[0].astype(F32)
    conv = cb_ref[...] + cw_ref[0:1, :] * xe[5:5 + ts, :]
    for kk in range(1, SSD_CONV):
        conv = conv + cw_ref[kk:kk + 1, :] * xe[5 + kk:5 + kk + ts, :]
    tail = xe[ts:ts + 8, :]
    xe[0:8, :] = tail
    xc = _silu(conv)
    xs = xc[:, :D_MODEL]
    bm = xc[:, D_MODEL:D_MODEL + SSD_GROUPS * SSD_STATE].astype(BF16)
    cm = xc[:, D_MODEL + SSD_GROUPS * SSD_STATE:].astype(BF16)

    dt = jax.nn.softplus(dt_ref[0] + dtb_ref[...])
    a = -jnp.exp(alog_ref[...]) * dt
    acum = _dot3_rhs(tril_ref[...], a)
    colexp = _dot3_lhs(acum, expand_ref[...])
    dtexp = _dot3_lhs(dt, expand_ref[...])
    rowexp = _dot3_rhs(ones_ref[...], colexp * eye_ref[...])
    lmat = jnp.exp(jnp.where(trilrep_ref[...] > 0.5, colexp - rowexp, NEG_INF))
    xdt = xs * dtexp
    decay_in = jnp.exp(colexp)

    bdmask = bd_ref[...] > 0.5
    ys = []
    for c in range(ts // CHUNK):
        rows = slice(c * CHUNK, (c + 1) * CHUNK)
        a_last = colexp[(c + 1) * CHUNK - 1:(c + 1) * CHUNK, :]
        decay_st = jnp.exp(a_last - colexp[rows, :])
        chunk_decay = jnp.exp(a_last)
        xd = (xdt[rows, :] * decay_st).astype(BF16)
        xb = xdt[rows, :].astype(BF16)
        yg = []
        for g in range(SSD_GROUPS):
            lanes = slice(g * GROUP_LANES, (g + 1) * GROUP_LANES)
            cg = cm[rows, g * SSD_STATE:(g + 1) * SSD_STATE]
            bg = bm[rows, g * SSD_STATE:(g + 1) * SSD_STATE]
            cb = _dot_nt(cg, jnp.concatenate([bg] * 4, axis=0))
            w = (cb * lmat[rows, lanes]).astype(BF16)
            xg = xb[:, lanes]
            xbd = jnp.where(bdmask, jnp.concatenate([xg] * 4, axis=0), jnp.zeros((), BF16))
            y_diag = _dot(w, xbd)
            prev = state[g]
            y_off = _dot(cg, prev.astype(BF16)) * decay_in[rows, lanes]
            state[g] = prev * chunk_decay[:, lanes] + _dot_tn(bg, xd[:, lanes])
            yg.append(y_diag + y_off)
        ys.append(jnp.concatenate(yg, axis=1))
    y = jnp.concatenate(ys, axis=0) + dexp_ref[...] * xs
    y = y * _silu(z_ref[0].astype(F32))
    o_ref[0] = _rms(y, g_ref[...]).astype(BF16)


def _ssd_constants(ts):
    r = jnp.arange(ts)
    same_chunk = (r[:, None] // CHUNK) == (r[None, :] // CHUNK)
    tril = (same_chunk & (r[:, None] >= r[None, :])).astype(BF16)
    ones = same_chunk.astype(BF16)
    lane = jnp.arange(D_MODEL)
    expand = (jnp.arange(LANES)[:, None] == (lane[None, :] // SSD_HEAD_DIM)).astype(BF16)
    l_in = r[:, None] % CHUNK
    s_in = lane[None, :] % SSD_HEAD_DIM
    eye = (l_in == s_in).astype(F32)
    trilrep = (l_in >= s_in).astype(F32)
    q = jnp.arange(GROUP_LANES)
    bd = ((q[:, None] // SSD_HEAD_DIM) == (q[None, :] // SSD_HEAD_DIM)).astype(F32)
    return tril, ones, expand, eye, trilrep, bd


def _ssd(proj3, dt3, conv_w, conv_b, dt_bias, a_log, d_exp, g_ssd, ts=256):
    b, s, _ = proj3.shape
    consts = _ssd_constants(ts)
    full = lambda arr: pl.BlockSpec(arr.shape, lambda bi, ti: (0,) * arr.ndim)
    params = (conv_w, conv_b, dt_bias, a_log, d_exp, g_ssd)
    return pl.pallas_call(
        functools.partial(_ssd_body, ts),
        grid=(b, s // ts),
        in_specs=[
            pl.BlockSpec((1, ts, D_MODEL), lambda bi, ti: (bi, ti, COL_Z // D_MODEL)),
            pl.BlockSpec((1, ts, SSD_XBC), lambda bi, ti: (bi, ti, COL_XBC // SSD_XBC)),
            pl.BlockSpec((1, ts, LANES), lambda bi, ti: (bi, ti, 0)),
        ] + [full(p) for p in params] + [full(c) for c in consts],
        out_specs=pl.BlockSpec((1, ts, D_MODEL), lambda bi, ti: (bi, ti, 0)),
        out_shape=jax.ShapeDtypeStruct((b, s, D_MODEL), BF16),
        scratch_shapes=[
            pltpu.VMEM((ts + 8, SSD_XBC), F32),
            pltpu.VMEM((SSD_GROUPS, SSD_STATE, GROUP_LANES), F32),
        ],
        compiler_params=_cparams(("arbitrary", "arbitrary"), 48),
        name="ssd_mixer",
    )(proj3, proj3, dt3, *params, *consts)


def _merge_body(att_ref, ssd_ref, ga_ref, gb_ref, x_ref, wa_ref, wb_ref, wo_ref, o_ref):
    ya = _dot(att_ref[...], wa_ref[...])
    yb = _dot(ssd_ref[...], wb_ref[...])
    ga = jax.nn.sigmoid(ga_ref[...].astype(F32))
    gb = jax.nn.sigmoid(gb_ref[...].astype(F32))
    m = (ga * ya + gb * yb).astype(BF16)
    o_ref[...] = x_ref[...] + _dot(m, wo_ref[...])


def _merge(att2, ssd2, proj, x2, wa, wb, wo, tm=512):
    t = x2.shape[0]
    row = lambda c: pl.BlockSpec((tm, D_MODEL), lambda i: (i, c))
    wspec = pl.BlockSpec((D_MODEL, D_MODEL), lambda i: (0, 0))
    gcol = COL_GATE // D_MODEL
    return pl.pallas_call(
        _merge_body,
        grid=(t // tm,),
        in_specs=[row(0), row(0), row(gcol), row(gcol + 1), row(0), wspec, wspec, wspec],
        out_specs=row(0),
        out_shape=jax.ShapeDtypeStruct((t, D_MODEL), F32),
        compiler_params=_cparams(("arbitrary",), 40),
        name="gated_merge",
    )(att2, ssd2, proj, proj, x2, wa, wb, wo)


SUBLANES = 8


def _batcher_network(n):
    def merge(lo, hi, r):
        step = r * 2
        if step < hi - lo:
            yield from merge(lo, hi, step)
            yield from merge(lo + r, hi, step)
            yield from [(i, i + r) for i in range(lo + r, hi - r, step)]
        else:
            yield (lo, lo + r)

    def sort(lo, hi):
        if hi - lo >= 1:
            mid = lo + (hi - lo) // 2
            yield from sort(lo, mid)
            yield from sort(mid + 1, hi)
            yield from merge(lo, hi, 1)

    return tuple(sort(0, n - 1))


_SORT16 = _batcher_network(PEER_TOPK)


def _exchange(xs, i, j):
    xs[i], xs[j] = jnp.maximum(xs[i], xs[j]), jnp.minimum(xs[i], xs[j])


def _all_sublanes(x, op):
    for shift in (4, 2, 1):
        x = op(x, pltpu.roll(x, shift, 0))
    return x


def _top16(s):
    xs = [s[SUBLANES * v:SUBLANES * (v + 1), :] for v in range(PEER_TOPK)]
    for i, j in _SORT16:
        _exchange(xs, i, j)
    for shift in (4, 2, 1):
        ys = [pltpu.roll(x, shift, 0) for x in xs]
        xs = [jnp.maximum(xs[k], ys[PEER_TOPK - 1 - k]) for k in range(PEER_TOPK)]
        for d in (8, 4, 2, 1):
            for i in range(PEER_TOPK):
                if i & d == 0:
                    _exchange(xs, i, i + d)
    return xs


def _stack(rows, sub):
    out = rows[SUBLANES - 1]
    for r in range(SUBLANES - 2, -1, -1):
        out = jnp.where(sub == r, rows[r], out)
    return out


def _candidate_groups(a_rep, a_hi, b_lo, b_hi, b0, combine):
    groups = [combine(a_rep[0], b_lo), combine(a_rep[0], b_hi)]
    groups += [combine(a_rep[a], b_lo) for a in range(1, SUBLANES)]
    groups.append(combine(a_hi, b0))
    return groups


def _route_body(x_ref, g_ref, wq_ref, keys_ref, ht_ref, a_ref, b_ref, tau_ref):
    hb = _rms(x_ref[...], g_ref[...])
    ht = hb.T.astype(BF16)
    ht_ref[...] = ht
    qt = _dot(wq_ref[...], ht).astype(BF16)
    tt = ht.shape[1]
    sub = lax.broadcasted_iota(jnp.int32, (SUBLANES, tt), 0)
    for n in range(PEER_HEADS):
        s1 = _dot(keys_ref[2 * n], qt[(2 * n) * PEER_HALF:(2 * n + 1) * PEER_HALF, :])
        s2 = _dot(keys_ref[2 * n + 1], qt[(2 * n + 1) * PEER_HALF:(2 * n + 2) * PEER_HALF, :])
        r1 = _top16(s1)
        r2 = _top16(s2)
        add = lambda p, q: p + q
        cand = _candidate_groups(r1, _stack(r1[SUBLANES:], sub), _stack(r2[:SUBLANES], sub),
                                 _stack(r2[SUBLANES:], sub), r2[0], add)
        work = cand
        for it in range(PEER_TOPK):
            m = functools.reduce(jnp.maximum, work)
            m = _all_sublanes(m, jnp.maximum)
            if it < PEER_TOPK - 1:
                work = [jnp.where(w == m, NEG_INF, w) for w in work]
        sel = [c >= m for c in cand]
        top = r1[0] + r2[0]
        z = functools.reduce(add, [jnp.where(s, jnp.exp(c - top), 0.0) for s, c in zip(sel, cand)])
        inv_z = 1.0 / _all_sublanes(z, add)
        a_ref[n] = jnp.exp(s1 - r1[0][0:1]) * inv_z[0:1]
        b_ref[n] = jnp.exp(s2 - r2[0][0:1])
        pa = [jnp.exp(r - r1[0]) * inv_z for r in r1]
        pb = [jnp.exp(r - r2[0]) for r in r2]
        prod = _candidate_groups(pa, _stack(pa[SUBLANES:], sub), _stack(pb[:SUBLANES], sub),
                                 _stack(pb[SUBLANES:], sub), pb[0], lambda p, q: p * q)
        tau = functools.reduce(jnp.minimum, [jnp.where(s, p, jnp.inf) for s, p in zip(sel, prod)])
        tau_ref[n] = _all_sublanes(tau, jnp.minimum)[0:1]


def _route(x1, g_ffn, wq_t, keys, tt=256):
    t = x1.shape[0]
    return pl.pallas_call(
        _route_body,
        grid=(t // tt,),
        in_specs=[
            pl.BlockSpec((tt, D_MODEL), lambda i: (i, 0)),
            pl.BlockSpec((1, D_MODEL), lambda i: (0, 0)),
            pl.BlockSpec(wq_t.shape, lambda i: (0, 0)),
            pl.BlockSpec(keys.shape, lambda i: (0, 0, 0)),
        ],
        out_specs=[
            pl.BlockSpec((D_MODEL, tt), lambda i: (0, i)),
            pl.BlockSpec((PEER_HEADS, PEER_NKEYS, tt), lambda i: (0, 0, i)),
            pl.BlockSpec((PEER_HEADS, PEER_NKEYS, tt), lambda i: (0, 0, i)),
            pl.BlockSpec((PEER_HEADS, 1, tt), lambda i: (0, 0, i)),
        ],
        out_shape=[
            jax.ShapeDtypeStruct((D_MODEL, t), BF16),
            jax.ShapeDtypeStruct((PEER_HEADS, PEER_NKEYS, t), F32),
            jax.ShapeDtypeStruct((PEER_HEADS, PEER_NKEYS, t), F32),
            jax.ShapeDtypeStruct((PEER_HEADS, 1, t), F32),
        ],
        compiler_params=_cparams(("arbitrary",), 48),
        name="peer_route",
    )(x1, g_ffn, wq_t, keys)


def _gelu(x):
    return 0.5 * x * (1.0 + lax.erf(x * (1.0 / math.sqrt(2.0))))


def _experts_body(te, ht_ref, u_ref, vt_ref, a_ref, b_ref, tau_ref, x_ref, g_ref, o_ref, acc, gates, act):
    e = pl.program_id(1)

    @pl.when(e == 0)
    def _():
        acc[...] = jnp.zeros(acc.shape, F32)

    nblk = te // PEER_NKEYS
    tt = ht_ref.shape[1]
    a_rows = [a_ref[n, pl.ds(pl.multiple_of(e * nblk, nblk), nblk), :] for n in range(PEER_HEADS)]
    for ib in range(nblk):
        rows = slice(ib * PEER_NKEYS, (ib + 1) * PEER_NKEYS)
        for lb in range(tt // LANES):
            lanes = slice(lb * LANES, (lb + 1) * LANES)
            gate = None
            for n in range(PEER_HEADS):
                p = a_rows[n][ib:ib + 1, lanes] * b_ref[n, :, lanes]
                term = jnp.where(p >= tau_ref[n, :, lanes], p, 0.0)
                gate = term if gate is None else gate + term
            gates[rows, lanes] = gate
    scores = _dot(u_ref[...], ht_ref[...])
    act[...] = (gates[...] * _gelu(scores)).astype(BF16)
    acc[...] += _dot(vt_ref[...], act[...])

    @pl.when(e == pl.num_programs(1) - 1)
    def _():
        o_ref[...] = _rms(x_ref[...] + acc[...].T, g_ref[...])


def _experts(h2t, u_b, vt_b, a_f, b_f, tau, x1, g_final, tt=512, te=1024):
    t = x1.shape[0]
    return pl.pallas_call(
        functools.partial(_experts_body, te),
        grid=(t // tt, PEER_EXPERTS // te),
        in_specs=[
            pl.BlockSpec((D_MODEL, tt), lambda i, e: (0, i)),
            pl.BlockSpec((te, D_MODEL), lambda i, e: (e, 0)),
            pl.BlockSpec((D_MODEL, te), lambda i, e: (0, e)),
            pl.BlockSpec((PEER_HEADS, PEER_NKEYS, tt), lambda i, e: (0, 0, i)),
            pl.BlockSpec((PEER_HEADS, PEER_NKEYS, tt), lambda i, e: (0, 0, i)),
            pl.BlockSpec((PEER_HEADS, 1, tt), lambda i, e: (0, 0, i)),
            pl.BlockSpec((tt, D_MODEL), lambda i, e: (i, 0)),
            pl.BlockSpec((1, D_MODEL), lambda i, e: (0, 0)),
        ],
        out_specs=pl.BlockSpec((tt, D_MODEL), lambda i, e: (i, 0)),
        out_shape=jax.ShapeDtypeStruct((t, D_MODEL), F32),
        scratch_shapes=[pltpu.VMEM((D_MODEL, tt), F32), pltpu.VMEM((te, tt), F32),
                        pltpu.VMEM((te, tt), BF16)],
        compiler_params=_cparams(("arbitrary", "arbitrary"), 56),
        name="peer_experts",
    )(h2t, u_b, vt_b, a_f, b_f, tau, x1, g_final)


def _lambda_init(layer_index):
    return 0.8 - 0.6 * math.exp(-0.3 * (layer_index - 1))


def _pad_lanes(v):
    return jnp.pad(v.reshape(1, -1), ((0, 0), (0, LANES - v.shape[-1])))


def _layer(x, layer, g_mix, w_in, lam_q1, lam_k1, lam_q2, lam_k2, g_subln, conv_w, conv_b, dt_bias,
           a_log, d_skip, g_ssd, w_branch_a, w_branch_b, w_out, g_ffn, w_query, sub_keys,
           expert_u, expert_v, g_out):
    b, s, d = x.shape
    t = b * s
    x2 = x.reshape(t, d)
    row = lambda v: v.reshape(1, -1)

    w = w_in[layer]
    dt_lo = COL_GATE
    q_scale = DA_HEAD_DIM ** -0.5 * LOG2E
    w_main = jnp.concatenate([w[:, :COL_K] * q_scale, w[:, COL_K:dt_lo], w[:, dt_lo + SSD_HEADS:]],
                             axis=1).astype(BF16)
    w_dt = jnp.pad(w[:, dt_lo:dt_lo + SSD_HEADS], ((0, 0), (0, LANES - SSD_HEADS))).astype(BF16)
    slopes = jnp.exp2(-8.0 * jnp.arange(1, DA_HEADS + 1, dtype=F32) / DA_HEADS)
    slopes = jnp.broadcast_to(slopes[:, None, None], (DA_HEADS, 1, LANES))

    proj, dt_raw = _inproj(x2, row(g_mix[layer]), w_main, w_dt)
    proj3 = proj.reshape(b, s, N_MAIN)

    att = _attention(proj3, slopes, row(lam_q1[layer]), row(lam_k1[layer]), row(lam_q2[layer]),
                     row(lam_k2[layer]), row(g_subln[layer]), _lambda_init(layer + 1))
    ssd = _ssd(proj3, dt_raw.reshape(b, s, LANES), conv_w[layer].reshape(SSD_CONV, SSD_XBC),
               row(conv_b[layer]), _pad_lanes(dt_bias[layer]), _pad_lanes(a_log[layer]),
               row(jnp.repeat(d_skip[layer], SSD_HEAD_DIM)), row(g_ssd[layer]))

    x1 = _merge(att.reshape(t, d), ssd.reshape(t, d), proj, x2, w_branch_a[layer].astype(BF16),
                w_branch_b[layer].astype(BF16), w_out[layer].astype(BF16))

    keys = sub_keys[layer].reshape(2 * PEER_HEADS, PEER_NKEYS, PEER_HALF).astype(BF16)
    h2, a_f, b_f, tau = _route(x1, row(g_ffn[layer]), w_query[layer].T.astype(BF16), keys)
    out = _experts(h2, expert_u[layer].astype(BF16), expert_v[layer].T.astype(BF16), a_f, b_f, tau,
                   x1, row(g_out))
    return out.reshape(b, s, d)


def kernel(x, g_mix, w_in, lam_q1, lam_k1, lam_q2, lam_k2, g_subln, conv_w, conv_b, dt_bias, a_log,
           d_skip, g_ssd, w_branch_a, w_branch_b, w_out, g_ffn, w_query, sub_keys, expert_u, expert_v,
           g_final):
    depth = g_mix.shape[0]
    assert depth == 1, "the final RMSNorm is fused into the last (only) layer's expert kernel"
    return _layer(x, 0, g_mix, w_in, lam_q1, lam_k1, lam_q2, lam_k2, g_subln, conv_w, conv_b, dt_bias,
                  a_log, d_skip, g_ssd, w_branch_a, w_branch_b, w_out, g_ffn, w_query, sub_keys,
                  expert_u, expert_v, g_final)
```

```python
import functools
import math

import jax
import jax.numpy as jnp
from jax import lax
from jax.experimental import pallas as pl
from jax.experimental.pallas import tpu as pltpu

F32 = jnp.float32
BF16 = jnp.bfloat16

D_MODEL = 1024
CHUNK = 64
EPS = 1e-6
LOG2E = 1.4426950408889634

DA_HEADS = 8
DA_HEAD_DIM = 64
DA_V_DIM = 128

SSD_HEADS = 16
SSD_HEAD_DIM = 64
SSD_GROUPS = 4
SSD_STATE = 128
SSD_CONV = 4
SSD_XBC = 2048
GROUP_LANES = (SSD_HEADS // SSD_GROUPS) * SSD_HEAD_DIM

PEER_HEADS = 8
PEER_NKEYS = 128
PEER_EXPERTS = PEER_NKEYS * PEER_NKEYS
PEER_HALF = 128
PEER_TOPK = 16

N_MAIN = 8192
COL_Q, COL_K, COL_V, COL_Z, COL_XBC, COL_GATE = 0, 1024, 2048, 3072, 4096, 6144
LANES = 128

NEG_INF = float("-inf")


def _cparams(sem, vmem_mb):
    return pltpu.CompilerParams(dimension_semantics=sem, vmem_limit_bytes=vmem_mb * 1024 * 1024)


def _rms(x, g):
    return x * lax.rsqrt(jnp.mean(x * x, axis=-1, keepdims=True) + EPS) * g


def _dot(a, b):
    return jnp.dot(a, b, preferred_element_type=F32)


def _dot_nt(a, b):
    return lax.dot_general(a, b, (((1,), (1,)), ((), ())), preferred_element_type=F32)


def _dot_tn(a, b):
    return lax.dot_general(a, b, (((0,), (0,)), ((), ())), preferred_element_type=F32)


def _inproj_body(x_ref, g_ref, w_ref, wdt_ref, o_ref, dt_ref, h_scr):
    @pl.when(pl.program_id(1) == 0)
    def _():
        hb = _rms(x_ref[...], g_ref[...]).astype(BF16)
        h_scr[...] = hb
        dt_ref[...] = _dot(hb, wdt_ref[...])

    o_ref[...] = _dot(h_scr[...], w_ref[...]).astype(BF16)


def _inproj(x2, g_mix, w_main, w_dt, tm=1024, tn=1024):
    t = x2.shape[0]
    return pl.pallas_call(
        _inproj_body,
        grid=(t // tm, N_MAIN // tn),
        in_specs=[
            pl.BlockSpec((tm, D_MODEL), lambda i, j: (i, 0)),
            pl.BlockSpec((1, D_MODEL), lambda i, j: (0, 0)),
            pl.BlockSpec((D_MODEL, tn), lambda i, j: (0, j)),
            pl.BlockSpec((D_MODEL, LANES), lambda i, j: (0, 0)),
        ],
        out_specs=[
            pl.BlockSpec((tm, tn), lambda i, j: (i, j)),
            pl.BlockSpec((tm, LANES), lambda i, j: (i, 0)),
        ],
        out_shape=[
            jax.ShapeDtypeStruct((t, N_MAIN), BF16),
            jax.ShapeDtypeStruct((t, LANES), F32),
        ],
        scratch_shapes=[pltpu.VMEM((tm, D_MODEL), BF16)],
        compiler_params=_cparams(("arbitrary", "arbitrary"), 40),
        name="inproj",
    )(x2, g_mix, w_main, w_dt)


ATT_ROWS = DA_V_DIM + 16


def _attn_body(lam0, tq, slope_ref, lq1_ref, lk1_ref, lq2_ref, lk2_ref, gsub_ref,
               q_ref, k_ref, v_ref, o_ref,
               bdiag, ka, kb_, vt, acc1, acc2, ua, ub):
    qi = pl.program_id(2)
    slope2 = slope_ref[0][:, 0:1] * LOG2E
    n_kv = v_ref.shape[1] // tq

    @pl.when(qi == 0)
    def _():
        jj = lax.broadcasted_iota(jnp.int32, (tq, tq), 0)
        ii = lax.broadcasted_iota(jnp.int32, (tq, tq), 1)
        d = (ii - jj).astype(F32)
        visible = (jj // CHUNK) <= (ii // CHUNK)
        bdiag[...] = jnp.where(visible, -slope2 * (jnp.abs(d) - d), NEG_INF)

        lane = lax.broadcasted_iota(jnp.int32, (tq, LANES), 1)
        c = slope2 * lax.broadcasted_iota(jnp.int32, (tq, LANES), 0).astype(F32)
        c_hi = c.astype(BF16).astype(F32)
        c_mid = (c - c_hi).astype(BF16).astype(F32)
        c_lo = c - c_hi - c_mid

        def extras(base):
            return jnp.where(lane == base, c_hi, jnp.where(lane == base + 1, c_mid,
                             jnp.where(lane == base + 2, c_lo, 0.0)))

        ext_a, ext_b = extras(DA_HEAD_DIM), extras(0)
        ones_row = lax.broadcasted_iota(jnp.int32, (ATT_ROWS - DA_V_DIM, tq), 0) == 0
        for cblk in range(n_kv):
            rows = slice(cblk * tq, (cblk + 1) * tq)
            kblk = k_ref[0, rows, :].astype(F32)
            ka[rows, :] = jnp.where(lane < DA_HEAD_DIM, kblk, ext_a).astype(BF16)
            kb_[rows, :] = jnp.where(lane >= DA_HEAD_DIM, kblk, ext_b).astype(BF16)
            vt[0:DA_V_DIM, rows] = v_ref[0, rows, :].astype(F32).T.astype(BF16)
            vt[DA_V_DIM:, rows] = jnp.where(ones_row, 1.0, 0.0).astype(BF16)

    qt = q_ref[0].astype(F32).T
    row = lax.broadcasted_iota(jnp.int32, qt.shape, 0)
    qa = jnp.where(row < DA_HEAD_DIM, qt, jnp.where(row < DA_HEAD_DIM + 3, 1.0, 0.0)).astype(BF16)
    qb = jnp.where(row >= DA_HEAD_DIM, qt, jnp.where(row < 3, 1.0, 0.0)).astype(BF16)

    streams = ((ka, qa, acc1), (kb_, qb, acc2))

    def scores(start, rows, dst):
        for mi, (kref, qm, _) in enumerate(streams):
            dst[mi] = _dot(kref[pl.ds(start, rows), :], qm)

    def absorb(src, start, bias, shift, ms):
        v_t = vt[:, pl.ds(start, hk)]
        out = []
        for mi, ((_, _, acc), m_old) in enumerate(zip(streams, ms)):
            u = src[mi]
            if bias is not None:
                u = u + bias
            m_new = jnp.maximum(m_old, jnp.max(u, axis=0, keepdims=True) + shift)
            p = jnp.exp2(u - (m_new - shift))
            pv = _dot(v_t, p.astype(BF16))
            acc[...] = jnp.exp2(m_old - m_new) * acc[...] + pv
            out.append(m_new)
        return tuple(out)

    hk = tq // 2
    acc1[...] = jnp.zeros(acc1.shape, F32)
    acc2[...] = jnp.zeros(acc2.shape, F32)
    neg = jnp.full((1, tq), NEG_INF, F32)
    scores(0, hk, ua)

    def body(j, ms):
        s0 = pl.multiple_of(2 * j * hk, hk)
        shift = -slope2 * ((qi - j) * tq).astype(F32)
        scores(s0 + hk, hk, ub)
        ms = absorb(ua, s0, None, shift, ms)
        scores(s0 + 2 * hk, hk, ua)
        return absorb(ub, s0 + hk, None, shift, ms)

    ms = lax.fori_loop(0, qi, body, (neg, neg))
    diag = pl.multiple_of(qi * tq, tq)
    no_shift = jnp.zeros((1, 1), F32)
    scores(diag + hk, hk, ub)
    ms = absorb(ua, diag, bdiag[0:hk, :], no_shift, ms)
    absorb(ub, diag + hk, bdiag[hk:tq, :], no_shift, ms)

    lam = (jnp.exp(jnp.sum(lq1_ref[...] * lk1_ref[...], axis=-1, keepdims=True))
           - jnp.exp(jnp.sum(lq2_ref[...] * lk2_ref[...], axis=-1, keepdims=True)) + lam0)
    o = (acc1[0:DA_V_DIM, :] / acc1[DA_V_DIM:DA_V_DIM + 1, :]
         - lam * (acc2[0:DA_V_DIM, :] / acc2[DA_V_DIM:DA_V_DIM + 1, :]))
    o = o * lax.rsqrt(jnp.mean(o * o, axis=0, keepdims=True) + EPS)
    o_ref[0] = (o.T * (gsub_ref[...] * (1.0 - lam0))).astype(BF16)


def _attention(proj3, slopes, lq1, lk1, lq2, lk2, g_subln, lam0, tq=512):
    b, s, _ = proj3.shape
    kcol, vcol = COL_K // LANES, COL_V // LANES
    small = lambda shape: pl.BlockSpec(shape, lambda bi, h, qi: (0,) * len(shape))
    return pl.pallas_call(
        functools.partial(_attn_body, lam0, tq),
        grid=(b, DA_HEADS, s // tq),
        in_specs=[
            pl.BlockSpec((1, 1, LANES), lambda bi, h, qi: (h, 0, 0)),
            small((1, DA_HEAD_DIM)), small((1, DA_HEAD_DIM)),
            small((1, DA_HEAD_DIM)), small((1, DA_HEAD_DIM)),
            small((1, DA_V_DIM)),
            pl.BlockSpec((1, tq, LANES), lambda bi, h, qi: (bi, qi, h)),
            pl.BlockSpec((1, s, LANES), lambda bi, h, qi: (bi, 0, kcol + h)),
            pl.BlockSpec((1, s, LANES), lambda bi, h, qi: (bi, 0, vcol + h)),
        ],
        out_specs=pl.BlockSpec((1, tq, LANES), lambda bi, h, qi: (bi, qi, h)),
        out_shape=jax.ShapeDtypeStruct((b, s, DA_HEADS * DA_V_DIM), BF16),
        scratch_shapes=[
            pltpu.VMEM((tq, tq), F32),
            pltpu.VMEM((s, LANES), BF16), pltpu.VMEM((s, LANES), BF16),
            pltpu.VMEM((ATT_ROWS, s), BF16),
            pltpu.VMEM((ATT_ROWS, tq), F32), pltpu.VMEM((ATT_ROWS, tq), F32),
            pltpu.VMEM((2, tq // 2, tq), F32), pltpu.VMEM((2, tq // 2, tq), F32),
        ],
        compiler_params=_cparams(("arbitrary", "arbitrary", "arbitrary"), 40),
        name="diff_attention",
    )(slopes, lq1, lk1, lq2, lk2, g_subln, proj3, proj3, proj3)


def _split3(x):
    hi = x.astype(BF16)
    r = x - hi.astype(F32)
    mid = r.astype(BF16)
    lo = (r - mid.astype(F32)).astype(BF16)
    return hi, mid, lo


def _dot3_rhs(a_b, x):
    hi, mid, lo = _split3(x)
    return _dot(a_b, hi) + _dot(a_b, mid) + _dot(a_b, lo)


def _dot3_lhs(x, a_b):
    hi, mid, lo = _split3(x)
    return _dot(hi, a_b) + _dot(mid, a_b) + _dot(lo, a_b)


def _silu(x):
    return x * jax.nn.sigmoid(x)


def _ssd_body(ts, z_ref, xbc_ref, dt_ref, cw_ref, cb_ref, dtb_ref, alog_ref, dexp_ref, g_ref,
              tril_ref, ones_ref, expand_ref, eye_ref, trilrep_ref, bd_ref,
              o_ref, xe, state):
    ti = pl.program_id(1)

    @pl.when(ti == 0)
    def _():
        xe[0:8, :] = jnp.zeros((8, SSD_XBC), F32)
        state[...] = jnp.zeros(state.shape, F32)

    xe[8:8 + ts, :] = xbc_ref[0].astype(F32)
    conv = cb_ref[...] + cw_ref[0:1, :] * xe[5:5 + ts, :]
    for kk in range(1, SSD_CONV):
        conv = conv + cw_ref[kk:kk + 1, :] * xe[5 + kk:5 + kk + ts, :]
    tail = xe[ts:ts + 8, :]
    xe[0:8, :] = tail
    xc = _silu(conv)
    xs = xc[:, :D_MODEL]
    bm = xc[:, D_MODEL:D_MODEL + SSD_GROUPS * SSD_STATE].astype(BF16)
    cm = xc[:, D_MODEL + SSD_GROUPS * SSD_STATE:].astype(BF16)

    dt = jax.nn.softplus(dt_ref[0] + dtb_ref[...])
    a = -jnp.exp(alog_ref[...]) * dt
    acum = _dot3_rhs(tril_ref[...], a)
    colexp = _dot3_lhs(acum, expand_ref[...])
    dtexp = _dot3_lhs(dt, expand_ref[...])
    rowexp = _dot3_rhs(ones_ref[...], colexp * eye_ref[...])
    lmat = jnp.exp(jnp.where(trilrep_ref[...] > 0.5, colexp - rowexp, NEG_INF))
    xdt = xs * dtexp
    decay_in = jnp.exp(colexp)

    bdmask = bd_ref[...] > 0.5
    ys = []
    for c in range(ts // CHUNK):
        rows = slice(c * CHUNK, (c + 1) * CHUNK)
        a_last = colexp[(c + 1) * CHUNK - 1:(c + 1) * CHUNK, :]
        decay_st = jnp.exp(a_last - colexp[rows, :])
        chunk_decay = jnp.exp(a_last)
        xd = (xdt[rows, :] * decay_st).astype(BF16)
        xb = xdt[rows, :].astype(BF16)
        yg = []
        for g in range(SSD_GROUPS):
            lanes = slice(g * GROUP_LANES, (g + 1) * GROUP_LANES)
            cg = cm[rows, g * SSD_STATE:(g + 1) * SSD_STATE]
            bg = bm[rows, g * SSD_STATE:(g + 1) * SSD_STATE]
            cb = _dot_nt(cg, jnp.concatenate([bg] * 4, axis=0))
            w = (cb * lmat[rows, lanes]).astype(BF16)
            xg = xb[:, lanes]
            xbd = jnp.where(bdmask, jnp.concatenate([xg] * 4, axis=0), jnp.zeros((), BF16))
            y_diag = _dot(w, xbd)
            prev = state[g]
            y_off = _dot(cg, prev.astype(BF16)) * decay_in[rows, lanes]
            state[g] = prev * chunk_decay[:, lanes] + _dot_tn(bg, xd[:, lanes])
            yg.append(y_diag + y_off)
        ys.append(jnp.concatenate(yg, axis=1))
    y = jnp.concatenate(ys, axis=0) + dexp_ref[...] * xs
    y = y * _silu(z_ref[0].astype(F32))
    o_ref[0] = _rms(y, g_ref[...]).astype(BF16)


def _ssd_constants(ts):
    r = jnp.arange(ts)
    same_chunk = (r[:, None] // CHUNK) == (r[None, :] // CHUNK)
    tril = (same_chunk & (r[:, None] >= r[None, :])).astype(BF16)
    ones = same_chunk.astype(BF16)
    lane = jnp.arange(D_MODEL)
    expand = (jnp.arange(LANES)[:, None] == (lane[None, :] // SSD_HEAD_DIM)).astype(BF16)
    l_in = r[:, None] % CHUNK
    s_in = lane[None, :] % SSD_HEAD_DIM
    eye = (l_in == s_in).astype(F32)
    trilrep = (l_in >= s_in).astype(F32)
    q = jnp.arange(GROUP_LANES)
    bd = ((q[:, None] // SSD_HEAD_DIM) == (q[None, :] // SSD_HEAD_DIM)).astype(F32)
    return tril, ones, expand, eye, trilrep, bd


def _ssd(proj3, dt3, conv_w, conv_b, dt_bias, a_log, d_exp, g_ssd, ts=256):
    b, s, _ = proj3.shape
    consts = _ssd_constants(ts)
    full = lambda arr: pl.BlockSpec(arr.shape, lambda bi, ti: (0,) * arr.ndim)
    params = (conv_w, conv_b, dt_bias, a_log, d_exp, g_ssd)
    return pl.pallas_call(
        functools.partial(_ssd_body, ts),
        grid=(b, s // ts),
        in_specs=[
            pl.BlockSpec((1, ts, D_MODEL), lambda bi, ti: (bi, ti, COL_Z // D_MODEL)),
            pl.BlockSpec((1, ts, SSD_XBC), lambda bi, ti: (bi, ti, COL_XBC // SSD_XBC)),
            pl.BlockSpec((1, ts, LANES), lambda bi, ti: (bi, ti, 0)),
        ] + [full(p) for p in params] + [full(c) for c in consts],
        out_specs=pl.BlockSpec((1, ts, D_MODEL), lambda bi, ti: (bi, ti, 0)),
        out_shape=jax.ShapeDtypeStruct((b, s, D_MODEL), BF16),
        scratch_shapes=[
            pltpu.VMEM((ts + 8, SSD_XBC), F32),
            pltpu.VMEM((SSD_GROUPS, SSD_STATE, GROUP_LANES), F32),
        ],
        compiler_params=_cparams(("arbitrary", "arbitrary"), 48),
        name="ssd_mixer",
    )(proj3, proj3, dt3, *params, *consts)


def _merge_body(att_ref, ssd_ref, ga_ref, gb_ref, x_ref, wa_ref, wb_ref, wo_ref, o_ref):
    ya = _dot(att_ref[...], wa_ref[...])
    yb = _dot(ssd_ref[...], wb_ref[...])
    ga = jax.nn.sigmoid(ga_ref[...].astype(F32))
    gb = jax.nn.sigmoid(gb_ref[...].astype(F32))
    m = (ga * ya + gb * yb).astype(BF16)
    o_ref[...] = x_ref[...] + _dot(m, wo_ref[...])


def _merge(att2, ssd2, proj, x2, wa, wb, wo, tm=512):
    t = x2.shape[0]
    row = lambda c: pl.BlockSpec((tm, D_MODEL), lambda i: (i, c))
    wspec = pl.BlockSpec((D_MODEL, D_MODEL), lambda i: (0, 0))
    gcol = COL_GATE // D_MODEL
    return pl.pallas_call(
        _merge_body,
        grid=(t // tm,),
        in_specs=[row(0), row(0), row(gcol), row(gcol + 1), row(0), wspec, wspec, wspec],
        out_specs=row(0),
        out_shape=jax.ShapeDtypeStruct((t, D_MODEL), F32),
        compiler_params=_cparams(("arbitrary",), 40),
        name="gated_merge",
    )(att2, ssd2, proj, proj, x2, wa, wb, wo)


SUBLANES = 8


def _batcher_network(n):
    def merge(lo, hi, r):
        step = r * 2
        if step < hi - lo:
            yield from merge(lo, hi, step)
            yield from merge(lo + r, hi, step)
            yield from [(i, i + r) for i in range(lo + r, hi - r, step)]
        else:
            yield (lo, lo + r)

    def sort(lo, hi):
        if hi - lo >= 1:
            mid = lo + (hi - lo) // 2
            yield from sort(lo, mid)
            yield from sort(mid + 1, hi)
            yield from merge(lo, hi, 1)

    return tuple(sort(0, n - 1))


_SORT16 = _batcher_network(PEER_TOPK)


def _exchange(xs, i, j):
    xs[i], xs[j] = jnp.maximum(xs[i], xs[j]), jnp.minimum(xs[i], xs[j])


def _all_sublanes(x, op):
    for shift in (4, 2, 1):
        x = op(x, pltpu.roll(x, shift, 0))
    return x


def _top16(s):
    xs = [s[SUBLANES * v:SUBLANES * (v + 1), :] for v in range(PEER_TOPK)]
    for i, j in _SORT16:
        _exchange(xs, i, j)
    for shift in (4, 2, 1):
        ys = [pltpu.roll(x, shift, 0) for x in xs]
        xs = [jnp.maximum(xs[k], ys[PEER_TOPK - 1 - k]) for k in range(PEER_TOPK)]
        for d in (8, 4, 2, 1):
            for i in range(PEER_TOPK):
                if i & d == 0:
                    _exchange(xs, i, i + d)
    return xs


def _stack(rows, sub):
    out = rows[SUBLANES - 1]
    for r in range(SUBLANES - 2, -1, -1):
        out = jnp.where(sub == r, rows[r], out)
    return out


def _candidate_groups(a_rep, a_hi, b_lo, b_hi, b0, combine):
    groups = [combine(a_rep[0], b_lo), combine(a_rep[0], b_hi)]
    groups += [combine(a_rep[a], b_lo) for a in range(1, SUBLANES)]
    groups.append(combine(a_hi, b0))
    return groups


def _route_body(x_ref, g_ref, wq_ref, keys_ref, ht_ref, a_ref, b_ref, tau_ref):
    hb = _rms(x_ref[...], g_ref[...])
    ht = hb.T.astype(BF16)
    ht_ref[...] = ht
    qt = _dot(wq_ref[...], ht).astype(BF16)
    tt = ht.shape[1]
    sub = lax.broadcasted_iota(jnp.int32, (SUBLANES, tt), 0)
    for n in range(PEER_HEADS):
        s1 = _dot(keys_ref[2 * n], qt[(2 * n) * PEER_HALF:(2 * n + 1) * PEER_HALF, :])
        s2 = _dot(keys_ref[2 * n + 1], qt[(2 * n + 1) * PEER_HALF:(2 * n + 2) * PEER_HALF, :])
        r1 = _top16(s1)
        r2 = _top16(s2)
        add = lambda p, q: p + q
        cand = _candidate_groups(r1, _stack(r1[SUBLANES:], sub), _stack(r2[:SUBLANES], sub),
                                 _stack(r2[SUBLANES:], sub), r2[0], add)
        work = cand
        for it in range(PEER_TOPK):
            m = functools.reduce(jnp.maximum, work)
            m = _all_sublanes(m, jnp.maximum)
            if it < PEER_TOPK - 1:
                work = [jnp.where(w == m, NEG_INF, w) for w in work]
        sel = [c >= m for c in cand]
        top = r1[0] + r2[0]
        z = functools.reduce(add, [jnp.where(s, jnp.exp(c - top), 0.0) for s, c in zip(sel, cand)])
        inv_z = 1.0 / _all_sublanes(z, add)
        a_ref[n] = jnp.exp(s1 - r1[0][0:1]) * inv_z[0:1]
        b_ref[n] = jnp.exp(s2 - r2[0][0:1])
        pa = [jnp.exp(r - r1[0]) * inv_z for r in r1]
        pb = [jnp.exp(r - r2[0]) for r in r2]
        prod = _candidate_groups(pa, _stack(pa[SUBLANES:], sub), _stack(pb[:SUBLANES], sub),
                                 _stack(pb[SUBLANES:], sub), pb[0], lambda p, q: p * q)
        tau = functools.reduce(jnp.minimum, [jnp.where(s, p, jnp.inf) for s, p in zip(sel, prod)])
        tau_ref[n] = _all_sublanes(tau, jnp.minimum)[0:1]


def _route(x1, g_ffn, wq_t, keys, tt=256):
    t = x1.shape[0]
    return pl.pallas_call(
        _route_body,
        grid=(t // tt,),
        in_specs=[
            pl.BlockSpec((tt, D_MODEL), lambda i: (i, 0)),
            pl.BlockSpec((1, D_MODEL), lambda i: (0, 0)),
            pl.BlockSpec(wq_t.shape, lambda i: (0, 0)),
            pl.BlockSpec(keys.shape, lambda i: (0, 0, 0)),
        ],
        out_specs=[
            pl.BlockSpec((D_MODEL, tt), lambda i: (0, i)),
            pl.BlockSpec((PEER_HEADS, PEER_NKEYS, tt), lambda i: (0, 0, i)),
            pl.BlockSpec((PEER_HEADS, PEER_NKEYS, tt), lambda i: (0, 0, i)),
            pl.BlockSpec((PEER_HEADS, 1, tt), lambda i: (0, 0, i)),
        ],
        out_shape=[
            jax.ShapeDtypeStruct((D_MODEL, t), BF16),
            jax.ShapeDtypeStruct((PEER_HEADS, PEER_NKEYS, t), F32),
            jax.ShapeDtypeStruct((PEER_HEADS, PEER_NKEYS, t), F32),
            jax.ShapeDtypeStruct((PEER_HEADS, 1, t), F32),
        ],
        compiler_params=_cparams(("arbitrary",), 48),
        name="peer_route",
    )(x1, g_ffn, wq_t, keys)


def _gelu(x):
    return 0.5 * x * (1.0 + lax.erf(x * (1.0 / math.sqrt(2.0))))


def _experts_body(te, ht_ref, u0_ref, un_ref, vt_ref, a_ref, b_ref, tau_ref, x_ref, g_ref, o_ref,
                  acc, s_even, s_odd, *act):
    e = pl.program_id(1)
    last = pl.num_programs(1) - 1
    nblk = te // PEER_NKEYS
    tt = ht_ref.shape[1]

    @pl.when(e == 0)
    def _():
        acc[...] = jnp.zeros(acc.shape, F32)
        s_even[...] = _dot(u0_ref[...], ht_ref[...])

    def step(s_cur, s_next):
        a_rows = [a_ref[n] for n in range(PEER_HEADS)]
        chunk = 2 * PEER_NKEYS
        wide = 2 * LANES
        for c in range(te // chunk):
            crow = slice(c * chunk, (c + 1) * chunk)
            for hb in range(tt // wide):
                cols = slice(hb * wide, (hb + 1) * wide)
                for ib in range(c * chunk // PEER_NKEYS, (c + 1) * chunk // PEER_NKEYS):
                    rows = slice(ib * PEER_NKEYS, (ib + 1) * PEER_NKEYS)
                    for lb in range(hb * wide // LANES, (hb + 1) * wide // LANES):
                        lanes = slice(lb * LANES, (lb + 1) * LANES)
                        gate = None
                        for n in range(PEER_HEADS):
                            p = a_rows[n][ib:ib + 1, lanes] * b_ref[n, :, lanes]
                            term = jnp.where(p >= tau_ref[n, :, lanes], p, 0.0)
                            gate = term if gate is None else gate + term
                        act[c][rows.start - crow.start:rows.stop - crow.start, lanes] = (
                            gate * _gelu(s_cur[rows, lanes])).astype(BF16)
                acc[:, cols] += _dot(vt_ref[:, crow], act[c][:, cols])
                s_next[crow, cols] = _dot(un_ref[crow, :], ht_ref[:, cols])

    @pl.when(e % 2 == 0)
    def _():
        step(s_even, s_odd)

    @pl.when(e % 2 == 1)
    def _():
        step(s_odd, s_even)

    @pl.when(e == last)
    def _():
        o_ref[...] = _rms(x_ref[...] + acc[...].T, g_ref[...])


def _experts(h2t, u_b, vt_b, a_f, b_f, tau, x1, g_final, tt=512, te=1024):
    t = x1.shape[0]
    n_e = PEER_EXPERTS // te
    return pl.pallas_call(
        functools.partial(_experts_body, te),
        grid=(t // tt, n_e),
        in_specs=[
            pl.BlockSpec((D_MODEL, tt), lambda i, e: (0, i)),
            pl.BlockSpec((te, D_MODEL), lambda i, e: (0, 0)),
            pl.BlockSpec((te, D_MODEL), lambda i, e: (jnp.minimum(e + 1, n_e - 1), 0)),
            pl.BlockSpec((D_MODEL, te), lambda i, e: (0, e)),
            pl.BlockSpec((PEER_HEADS, te // PEER_NKEYS, tt), lambda i, e: (0, e, i)),
            pl.BlockSpec((PEER_HEADS, PEER_NKEYS, tt), lambda i, e: (0, 0, i)),
            pl.BlockSpec((PEER_HEADS, 1, tt), lambda i, e: (0, 0, i)),
            pl.BlockSpec((tt, D_MODEL), lambda i, e: (i, 0)),
            pl.BlockSpec((1, D_MODEL), lambda i, e: (0, 0)),
        ],
        out_specs=pl.BlockSpec((tt, D_MODEL), lambda i, e: (i, 0)),
        out_shape=jax.ShapeDtypeStruct((t, D_MODEL), F32),
        scratch_shapes=[pltpu.VMEM((D_MODEL, tt), F32), pltpu.VMEM((te, tt), F32),
                        pltpu.VMEM((te, tt), F32)]
        + [pltpu.VMEM((2 * PEER_NKEYS, tt), BF16)] * (te // (2 * PEER_NKEYS)),
        compiler_params=_cparams(("arbitrary", "arbitrary"), 56),
        name="peer_experts",
    )(h2t, u_b, u_b, vt_b, a_f, b_f, tau, x1, g_final)


def _lambda_init(layer_index):
    return 0.8 - 0.6 * math.exp(-0.3 * (layer_index - 1))


def _pad_lanes(v):
    return jnp.pad(v.reshape(1, -1), ((0, 0), (0, LANES - v.shape[-1])))


def _layer(x, layer, g_mix, w_in, lam_q1, lam_k1, lam_q2, lam_k2, g_subln, conv_w, conv_b, dt_bias,
           a_log, d_skip, g_ssd, w_branch_a, w_branch_b, w_out, g_ffn, w_query, sub_keys,
           expert_u, expert_v, g_out):
    b, s, d = x.shape
    t = b * s
    x2 = x.reshape(t, d)
    row = lambda v: v.reshape(1, -1)

    w = w_in[layer]
    dt_lo = COL_GATE
    q_scale = DA_HEAD_DIM ** -0.5 * LOG2E
    w_main = jnp.concatenate([w[:, :COL_K] * q_scale, w[:, COL_K:dt_lo], w[:, dt_lo + SSD_HEADS:]],
                             axis=1).astype(BF16)
    w_dt = jnp.pad(w[:, dt_lo:dt_lo + SSD_HEADS], ((0, 0), (0, LANES - SSD_HEADS))).astype(BF16)
    slopes = jnp.exp2(-8.0 * jnp.arange(1, DA_HEADS + 1, dtype=F32) / DA_HEADS)
    slopes = jnp.broadcast_to(slopes[:, None, None], (DA_HEADS, 1, LANES))

    proj, dt_raw = _inproj(x2, row(g_mix[layer]), w_main, w_dt)
    proj3 = proj.reshape(b, s, N_MAIN)

    att = _attention(proj3, slopes, row(lam_q1[layer]), row(lam_k1[layer]), row(lam_q2[layer]),
                     row(lam_k2[layer]), row(g_subln[layer]), _lambda_init(layer + 1))
    ssd = _ssd(proj3, dt_raw.reshape(b, s, LANES), conv_w[layer].reshape(SSD_CONV, SSD_XBC),
               row(conv_b[layer]), _pad_lanes(dt_bias[layer]), _pad_lanes(a_log[layer]),
               row(jnp.repeat(d_skip[layer], SSD_HEAD_DIM)), row(g_ssd[layer]))

    x1 = _merge(att.reshape(t, d), ssd.reshape(t, d), proj, x2, w_branch_a[layer].astype(BF16),
                w_branch_b[layer].astype(BF16), w_out[layer].astype(BF16))

    keys = sub_keys[layer].reshape(2 * PEER_HEADS, PEER_NKEYS, PEER_HALF).astype(BF16)
    h2t, a_f, b_f, tau = _route(x1, row(g_ffn[layer]), w_query[layer].T.astype(BF16), keys)
    out = _experts(h2t, expert_u[layer].astype(BF16), expert_v[layer].T.astype(BF16), a_f, b_f, tau,
                   x1, row(g_out))
    return out.reshape(b, s, d)


def kernel(x, g_mix, w_in, lam_q1, lam_k1, lam_q2, lam_k2, g_subln, conv_w, conv_b, dt_bias, a_log,
           d_skip, g_ssd, w_branch_a, w_branch_b, w_out, g_ffn, w_query, sub_keys, expert_u, expert_v,
           g_final):
    depth = g_mix.shape[0]
    assert depth == 1, "the final RMSNorm is fused into the last (only) layer's expert kernel"
    return _layer(x, 0, g_mix, w_in, lam_q1, lam_k1, lam_q2, lam_k2, g_subln, conv_w, conv_b, dt_bias,
                  a_log, d_skip, g_ssd, w_branch_a, w_branch_b, w_out, g_ffn, w_query, sub_keys,
                  expert_u, expert_v, g_final)
```

```python
import functools
import math

import jax
import jax.numpy as jnp
from jax import lax
from jax.experimental import pallas as pl
from jax.experimental.pallas import tpu as pltpu

F32 = jnp.float32
BF16 = jnp.bfloat16

D_MODEL = 1024
CHUNK = 64
EPS = 1e-6
LOG2E = 1.4426950408889634

DA_HEADS = 8
DA_HEAD_DIM = 64
DA_V_DIM = 128

SSD_HEADS = 16
SSD_HEAD_DIM = 64
SSD_GROUPS = 4
SSD_STATE = 128
SSD_CONV = 4
SSD_XBC = 2048
GROUP_LANES = (SSD_HEADS // SSD_GROUPS) * SSD_HEAD_DIM

PEER_HEADS = 8
PEER_NKEYS = 128
PEER_EXPERTS = PEER_NKEYS * PEER_NKEYS
PEER_HALF = 128
PEER_TOPK = 16
GELU_ARG_SCALE = 1.0 / math.sqrt(2.0)
GELU_OUT_SCALE = 1.0 / math.sqrt(2.0)

N_MAIN = 8192
COL_Q, COL_K, COL_V, COL_Z, COL_XBC, COL_GATE = 0, 1024, 2048, 3072, 4096, 6144
LANES = 128

NEG_INF = float("-inf")


def _cparams(sem, vmem_mb):
    return pltpu.CompilerParams(dimension_semantics=sem, vmem_limit_bytes=vmem_mb * 1024 * 1024)


def _rms(x, g):
    return x * lax.rsqrt(jnp.mean(x * x, axis=-1, keepdims=True) + EPS) * g


def _dot(a, b):
    return jnp.dot(a, b, preferred_element_type=F32)


def _dot_nt(a, b):
    return lax.dot_general(a, b, (((1,), (1,)), ((), ())), preferred_element_type=F32)


def _dot_tn(a, b):
    return lax.dot_general(a, b, (((0,), (0,)), ((), ())), preferred_element_type=F32)


def _inproj_body(x_ref, g_ref, w_ref, wdt_ref, o_ref, dt_ref, h_scr):
    @pl.when(pl.program_id(1) == 0)
    def _():
        hb = _rms(x_ref[...], g_ref[...]).astype(BF16)
        h_scr[...] = hb
        dt_ref[...] = _dot(hb, wdt_ref[...])

    o_ref[...] = _dot(h_scr[...], w_ref[...]).astype(BF16)


def _inproj(x2, g_mix, w_main, w_dt, tm=1024, tn=1024):
    t = x2.shape[0]
    return pl.pallas_call(
        _inproj_body,
        grid=(t // tm, N_MAIN // tn),
        in_specs=[
            pl.BlockSpec((tm, D_MODEL), lambda i, j: (i, 0)),
            pl.BlockSpec((1, D_MODEL), lambda i, j: (0, 0)),
            pl.BlockSpec((D_MODEL, tn), lambda i, j: (0, j)),
            pl.BlockSpec((D_MODEL, LANES), lambda i, j: (0, 0)),
        ],
        out_specs=[
            pl.BlockSpec((tm, tn), lambda i, j: (i, j)),
            pl.BlockSpec((tm, LANES), lambda i, j: (i, 0)),
        ],
        out_shape=[
            jax.ShapeDtypeStruct((t, N_MAIN), BF16),
            jax.ShapeDtypeStruct((t, LANES), F32),
        ],
        scratch_shapes=[pltpu.VMEM((tm, D_MODEL), BF16)],
        compiler_params=_cparams(("arbitrary", "arbitrary"), 40),
        name="inproj",
    )(x2, g_mix, w_main, w_dt)


ATT_ROWS = DA_V_DIM + 16


def _attn_body(lam0, tq, slope_ref, lq1_ref, lk1_ref, lq2_ref, lk2_ref, gsub_ref,
               q_ref, k_ref, v_ref, o_ref,
               bdiag, ka, kb_, vt, acc1, acc2, ua, ub):
    qi = pl.program_id(2)
    slope2 = slope_ref[0][:, 0:1] * LOG2E
    n_kv = v_ref.shape[1] // tq

    @pl.when(qi == 0)
    def _():
        jj = lax.broadcasted_iota(jnp.int32, (tq, tq), 0)
        ii = lax.broadcasted_iota(jnp.int32, (tq, tq), 1)
        d = (ii - jj).astype(F32)
        visible = (jj // CHUNK) <= (ii // CHUNK)
        bdiag[...] = jnp.where(visible, -slope2 * (jnp.abs(d) - d), NEG_INF)

        lane = lax.broadcasted_iota(jnp.int32, (tq, LANES), 1)
        c = slope2 * lax.broadcasted_iota(jnp.int32, (tq, LANES), 0).astype(F32)
        c_hi = c.astype(BF16).astype(F32)
        c_mid = (c - c_hi).astype(BF16).astype(F32)
        c_lo = c - c_hi - c_mid

        def extras(base):
            return jnp.where(lane == base, c_hi, jnp.where(lane == base + 1, c_mid,
                             jnp.where(lane == base + 2, c_lo, 0.0)))

        ext_a, ext_b = extras(DA_HEAD_DIM), extras(0)
        ones_row = lax.broadcasted_iota(jnp.int32, (ATT_ROWS - DA_V_DIM, tq), 0) == 0
        for cblk in range(n_kv):
            rows = slice(cblk * tq, (cblk + 1) * tq)
            kblk = k_ref[0, rows, :].astype(F32)
            ka[rows, :] = jnp.where(lane < DA_HEAD_DIM, kblk, ext_a).astype(BF16)
            kb_[rows, :] = jnp.where(lane >= DA_HEAD_DIM, kblk, ext_b).astype(BF16)
            vt[0:DA_V_DIM, rows] = v_ref[0, rows, :].astype(F32).T.astype(BF16)
            vt[DA_V_DIM:, rows] = jnp.where(ones_row, 1.0, 0.0).astype(BF16)

    qt = q_ref[0].astype(F32).T
    row = lax.broadcasted_iota(jnp.int32, qt.shape, 0)
    qa = jnp.where(row < DA_HEAD_DIM, qt, jnp.where(row < DA_HEAD_DIM + 3, 1.0, 0.0)).astype(BF16)
    qb = jnp.where(row >= DA_HEAD_DIM, qt, jnp.where(row < 3, 1.0, 0.0)).astype(BF16)

    streams = ((ka, qa, acc1), (kb_, qb, acc2))

    def scores(start, rows, dst):
        for mi, (kref, qm, _) in enumerate(streams):
            dst[mi] = _dot(kref[pl.ds(start, rows), :], qm)

    def absorb(src, start, bias, shift, ms):
        v_t = vt[:, pl.ds(start, hk)]
        out = []
        for mi, ((_, _, acc), m_old) in enumerate(zip(streams, ms)):
            u = src[mi]
            if bias is not None:
                u = u + bias
            m_new = jnp.maximum(m_old, jnp.max(u, axis=0, keepdims=True) + shift)
            p = jnp.exp2(u - (m_new - shift))
            pv = _dot(v_t, p.astype(BF16))
            acc[...] = jnp.exp2(m_old - m_new) * acc[...] + pv
            out.append(m_new)
        return tuple(out)

    hk = tq // 2
    acc1[...] = jnp.zeros(acc1.shape, F32)
    acc2[...] = jnp.zeros(acc2.shape, F32)
    neg = jnp.full((1, tq), NEG_INF, F32)
    scores(0, hk, ua)

    def body(j, ms):
        s0 = pl.multiple_of(2 * j * hk, hk)
        shift = -slope2 * ((qi - j) * tq).astype(F32)
        scores(s0 + hk, hk, ub)
        ms = absorb(ua, s0, None, shift, ms)
        scores(s0 + 2 * hk, hk, ua)
        return absorb(ub, s0 + hk, None, shift, ms)

    ms = lax.fori_loop(0, qi, body, (neg, neg))
    diag = pl.multiple_of(qi * tq, tq)
    no_shift = jnp.zeros((1, 1), F32)
    scores(diag + hk, hk, ub)
    ms = absorb(ua, diag, bdiag[0:hk, :], no_shift, ms)
    absorb(ub, diag + hk, bdiag[hk:tq, :], no_shift, ms)

    lam = (jnp.exp(jnp.sum(lq1_ref[...] * lk1_ref[...], axis=-1, keepdims=True))
           - jnp.exp(jnp.sum(lq2_ref[...] * lk2_ref[...], axis=-1, keepdims=True)) + lam0)
    o = (acc1[0:DA_V_DIM, :] / acc1[DA_V_DIM:DA_V_DIM + 1, :]
         - lam * (acc2[0:DA_V_DIM, :] / acc2[DA_V_DIM:DA_V_DIM + 1, :]))
    o = o * lax.rsqrt(jnp.mean(o * o, axis=0, keepdims=True) + EPS)
    o_ref[0] = (o.T * (gsub_ref[...] * (1.0 - lam0))).astype(BF16)


def _attention(proj3, slopes, lq1, lk1, lq2, lk2, g_subln, lam0, tq=512):
    b, s, _ = proj3.shape
    kcol, vcol = COL_K // LANES, COL_V // LANES
    small = lambda shape: pl.BlockSpec(shape, lambda bi, h, qi: (0,) * len(shape))
    return pl.pallas_call(
        functools.partial(_attn_body, lam0, tq),
        grid=(b, DA_HEADS, s // tq),
        in_specs=[
            pl.BlockSpec((1, 1, LANES), lambda bi, h, qi: (h, 0, 0)),
            small((1, DA_HEAD_DIM)), small((1, DA_HEAD_DIM)),
            small((1, DA_HEAD_DIM)), small((1, DA_HEAD_DIM)),
            small((1, DA_V_DIM)),
            pl.BlockSpec((1, tq, LANES), lambda bi, h, qi: (bi, qi, h)),
            pl.BlockSpec((1, s, LANES), lambda bi, h, qi: (bi, 0, kcol + h)),
            pl.BlockSpec((1, s, LANES), lambda bi, h, qi: (bi, 0, vcol + h)),
        ],
        out_specs=pl.BlockSpec((1, tq, LANES), lambda bi, h, qi: (bi, qi, h)),
        out_shape=jax.ShapeDtypeStruct((b, s, DA_HEADS * DA_V_DIM), BF16),
        scratch_shapes=[
            pltpu.VMEM((tq, tq), F32),
            pltpu.VMEM((s, LANES), BF16), pltpu.VMEM((s, LANES), BF16),
            pltpu.VMEM((ATT_ROWS, s), BF16),
            pltpu.VMEM((ATT_ROWS, tq), F32), pltpu.VMEM((ATT_ROWS, tq), F32),
            pltpu.VMEM((2, tq // 2, tq), F32), pltpu.VMEM((2, tq // 2, tq), F32),
        ],
        compiler_params=_cparams(("arbitrary", "arbitrary", "arbitrary"), 40),
        name="diff_attention",
    )(slopes, lq1, lk1, lq2, lk2, g_subln, proj3, proj3, proj3)


def _split3(x):
    hi = x.astype(BF16)
    r = x - hi.astype(F32)
    mid = r.astype(BF16)
    lo = (r - mid.astype(F32)).astype(BF16)
    return hi, mid, lo


def _dot3_rhs(a_b, x):
    hi, mid, lo = _split3(x)
    return _dot(a_b, hi) + _dot(a_b, mid) + _dot(a_b, lo)


def _dot3_lhs(x, a_b):
    hi, mid, lo = _split3(x)
    return _dot(hi, a_b) + _dot(mid, a_b) + _dot(lo, a_b)


def _silu(x):
    return x * jax.nn.sigmoid(x)


def _ssd_body(ts, z_ref, xbc_ref, dt_ref, cw_ref, cb_ref, dtb_ref, alog_ref, dexp_ref, g_ref,
              tril_ref, ones_ref, expand_ref, eye_ref, trilrep_ref, bd_ref,
              o_ref, xe, state):
    ti = pl.program_id(1)

    @pl.when(ti == 0)
    def _():
        xe[0:8, :] = jnp.zeros((8, SSD_XBC), F32)
        state[...] = jnp.zeros(state.shape, F32)

    xe[8:8 + ts, :] = xbc_ref[0].astype(F32)
    conv = cb_ref[...] + cw_ref[0:1, :] * xe[5:5 + ts, :]
    for kk in range(1, SSD_CONV):
        conv = conv + cw_ref[kk:kk + 1, :] * xe[5 + kk:5 + kk + ts, :]
    tail = xe[ts:ts + 8, :]
    xe[0:8, :] = tail
    xc = _silu(conv)
    xs = xc[:, :D_MODEL]
    bm = xc[:, D_MODEL:D_MODEL + SSD_GROUPS * SSD_STATE].astype(BF16)
    cm = xc[:, D_MODEL + SSD_GROUPS * SSD_STATE:].astype(BF16)

    dt = jax.nn.softplus(dt_ref[0] + dtb_ref[...])
    a = -jnp.exp(alog_ref[...]) * dt
    acum = _dot3_rhs(tril_ref[...], a)
    colexp = _dot3_lhs(acum, expand_ref[...])
    dtexp = _dot3_lhs(dt, expand_ref[...])
    rowexp = _dot3_rhs(ones_ref[...], colexp * eye_ref[...])
    lmat = jnp.exp(jnp.where(trilrep_ref[...] > 0.5, colexp - rowexp, NEG_INF))
    xdt = xs * dtexp
    decay_in = jnp.exp(colexp)

    bdmask = bd_ref[...] > 0.5
    ys = []
    for c in range(ts // CHUNK):
        rows = slice(c * CHUNK, (c + 1) * CHUNK)
        a_last = colexp[(c + 1) * CHUNK - 1:(c + 1) * CHUNK, :]
        decay_st = jnp.exp(a_last - colexp[rows, :])
        chunk_decay = jnp.exp(a_last)
        xd = (xdt[rows, :] * decay_st).astype(BF16)
        xb = xdt[rows, :].astype(BF16)
        yg = []
        for g in range(SSD_GROUPS):
            lanes = slice(g * GROUP_LANES, (g + 1) * GROUP_LANES)
            cg = cm[rows, g * SSD_STATE:(g + 1) * SSD_STATE]
            bg = bm[rows, g * SSD_STATE:(g + 1) * SSD_STATE]
            cb = _dot_nt(cg, jnp.concatenate([bg] * 4, axis=0))
            w = (cb * lmat[rows, lanes]).astype(BF16)
            xg = xb[:, lanes]
            xbd = jnp.where(bdmask, jnp.concatenate([xg] * 4, axis=0), jnp.zeros((), BF16))
            y_diag = _dot(w, xbd)
            prev = state[g]
            y_off = _dot(cg, prev.astype(BF16)) * decay_in[rows, lanes]
            state[g] = prev * chunk_decay[:, lanes] + _dot_tn(bg, xd[:, lanes])
            yg.append(y_diag + y_off)
        ys.append(jnp.concatenate(yg, axis=1))
    y = jnp.concatenate(ys, axis=0) + dexp_ref[...] * xs
    y = y * _silu(z_ref[0].astype(F32))
    o_ref[0] = _rms(y, g_ref[...]).astype(BF16)


def _ssd_constants(ts):
    r = jnp.arange(ts)
    same_chunk = (r[:, None] // CHUNK) == (r[None, :] // CHUNK)
    tril = (same_chunk & (r[:, None] >= r[None, :])).astype(BF16)
    ones = same_chunk.astype(BF16)
    lane = jnp.arange(D_MODEL)
    expand = (jnp.arange(LANES)[:, None] == (lane[None, :] // SSD_HEAD_DIM)).astype(BF16)
    l_in = r[:, None] % CHUNK
    s_in = lane[None, :] % SSD_HEAD_DIM
    eye = (l_in == s_in).astype(F32)
    trilrep = (l_in >= s_in).astype(F32)
    q = jnp.arange(GROUP_LANES)
    bd = ((q[:, None] // SSD_HEAD_DIM) == (q[None, :] // SSD_HEAD_DIM)).astype(F32)
    return tril, ones, expand, eye, trilrep, bd


def _ssd(proj3, dt3, conv_w, conv_b, dt_bias, a_log, d_exp, g_ssd, ts=256):
    b, s, _ = proj3.shape
    consts = _ssd_constants(ts)
    full = lambda arr: pl.BlockSpec(arr.shape, lambda bi, ti: (0,) * arr.ndim)
    params = (conv_w, conv_b, dt_bias, a_log, d_exp, g_ssd)
    return pl.pallas_call(
        functools.partial(_ssd_body, ts),
        grid=(b, s // ts),
        in_specs=[
            pl.BlockSpec((1, ts, D_MODEL), lambda bi, ti: (bi, ti, COL_Z // D_MODEL)),
            pl.BlockSpec((1, ts, SSD_XBC), lambda bi, ti: (bi, ti, COL_XBC // SSD_XBC)),
            pl.BlockSpec((1, ts, LANES), lambda bi, ti: (bi, ti, 0)),
        ] + [full(p) for p in params] + [full(c) for c in consts],
        out_specs=pl.BlockSpec((1, ts, D_MODEL), lambda bi, ti: (bi, ti, 0)),
        out_shape=jax.ShapeDtypeStruct((b, s, D_MODEL), BF16),
        scratch_shapes=[
            pltpu.VMEM((ts + 8, SSD_XBC), F32),
            pltpu.VMEM((SSD_GROUPS, SSD_STATE, GROUP_LANES), F32),
        ],
        compiler_params=_cparams(("arbitrary", "arbitrary"), 48),
        name="ssd_mixer",
    )(proj3, proj3, dt3, *params, *consts)


def _merge_body(att_ref, ssd_ref, ga_ref, gb_ref, x_ref, wa_ref, wb_ref, wo_ref, o_ref):
    ya = _dot(att_ref[...], wa_ref[...])
    yb = _dot(ssd_ref[...], wb_ref[...])
    ga = jax.nn.sigmoid(ga_ref[...].astype(F32))
    gb = jax.nn.sigmoid(gb_ref[...].astype(F32))
    m = (ga * ya + gb * yb).astype(BF16)
    o_ref[...] = x_ref[...] + _dot(m, wo_ref[...])


def _merge(att2, ssd2, proj, x2, wa, wb, wo, tm=512):
    t = x2.shape[0]
    row = lambda c: pl.BlockSpec((tm, D_MODEL), lambda i: (i, c))
    wspec = pl.BlockSpec((D_MODEL, D_MODEL), lambda i: (0, 0))
    gcol = COL_GATE // D_MODEL
    return pl.pallas_call(
        _merge_body,
        grid=(t // tm,),
        in_specs=[row(0), row(0), row(gcol), row(gcol + 1), row(0), wspec, wspec, wspec],
        out_specs=row(0),
        out_shape=jax.ShapeDtypeStruct((t, D_MODEL), F32),
        compiler_params=_cparams(("arbitrary",), 40),
        name="gated_merge",
    )(att2, ssd2, proj, proj, x2, wa, wb, wo)


SUBLANES = 8


def _batcher_network(n):
    def merge(lo, hi, r):
        step = r * 2
        if step < hi - lo:
            yield from merge(lo, hi, step)
            yield from merge(lo + r, hi, step)
            yield from [(i, i + r) for i in range(lo + r, hi - r, step)]
        else:
            yield (lo, lo + r)

    def sort(lo, hi):
        if hi - lo >= 1:
            mid = lo + (hi - lo) // 2
            yield from sort(lo, mid)
            yield from sort(mid + 1, hi)
            yield from merge(lo, hi, 1)

    return tuple(sort(0, n - 1))


_SORT16 = _batcher_network(PEER_TOPK)


def _exchange(xs, i, j):
    xs[i], xs[j] = jnp.maximum(xs[i], xs[j]), jnp.minimum(xs[i], xs[j])


def _all_sublanes(x, op):
    for shift in (4, 2, 1):
        x = op(x, pltpu.roll(x, shift, 0))
    return x


def _top16(s):
    xs = [s[SUBLANES * v:SUBLANES * (v + 1), :] for v in range(PEER_TOPK)]
    for i, j in _SORT16:
        _exchange(xs, i, j)
    for shift in (4, 2, 1):
        ys = [pltpu.roll(x, shift, 0) for x in xs]
        xs = [jnp.maximum(xs[k], ys[PEER_TOPK - 1 - k]) for k in range(PEER_TOPK)]
        for d in (8, 4, 2, 1):
            for i in range(PEER_TOPK):
                if i & d == 0:
                    _exchange(xs, i, i + d)
    return xs


def _stack(rows, sub):
    out = rows[SUBLANES - 1]
    for r in range(SUBLANES - 2, -1, -1):
        out = jnp.where(sub == r, rows[r], out)
    return out


def _candidate_groups(a_rep, a_hi, b_lo, b_hi, b0, combine):
    groups = [combine(a_rep[0], b_lo), combine(a_rep[0], b_hi)]
    groups += [combine(a_rep[a], b_lo) for a in range(1, SUBLANES)]
    groups.append(combine(a_hi, b0))
    return groups


def _route_body(x_ref, g_ref, wq_ref, keys_ref, ht_ref, a_ref, b_ref, tau_ref):
    hb = _rms(x_ref[...], g_ref[...])
    ht = hb.T.astype(BF16)
    ht_ref[...] = ht
    qt = _dot(wq_ref[...], ht).astype(BF16)
    tt = ht.shape[1]
    sub = lax.broadcasted_iota(jnp.int32, (SUBLANES, tt), 0)
    for n in range(PEER_HEADS):
        s1 = _dot(keys_ref[2 * n], qt[(2 * n) * PEER_HALF:(2 * n + 1) * PEER_HALF, :])
        s2 = _dot(keys_ref[2 * n + 1], qt[(2 * n + 1) * PEER_HALF:(2 * n + 2) * PEER_HALF, :])
        r1 = _top16(s1)
        r2 = _top16(s2)
        add = lambda p, q: p + q
        cand = _candidate_groups(r1, _stack(r1[SUBLANES:], sub), _stack(r2[:SUBLANES], sub),
                                 _stack(r2[SUBLANES:], sub), r2[0], add)
        work = cand
        for it in range(PEER_TOPK):
            m = functools.reduce(jnp.maximum, work)
            m = _all_sublanes(m, jnp.maximum)
            if it < PEER_TOPK - 1:
                work = [jnp.where(w == m, NEG_INF, w) for w in work]
        sel = [c >= m for c in cand]
        top = r1[0] + r2[0]
        z = functools.reduce(add, [jnp.where(s, jnp.exp(c - top), 0.0) for s, c in zip(sel, cand)])
        inv_z = GELU_OUT_SCALE / _all_sublanes(z, add)
        a_ref[n] = jnp.exp(s1 - r1[0][0:1]) * inv_z[0:1]
        b_ref[n] = jnp.exp(s2 - r2[0][0:1])
        pa = [jnp.exp(r - r1[0]) * inv_z for r in r1]
        pb = [jnp.exp(r - r2[0]) for r in r2]
        prod = _candidate_groups(pa, _stack(pa[SUBLANES:], sub), _stack(pb[:SUBLANES], sub),
                                 _stack(pb[SUBLANES:], sub), pb[0], lambda p, q: p * q)
        tau = functools.reduce(jnp.minimum, [jnp.where(s, p, jnp.inf) for s, p in zip(sel, prod)])
        tau_ref[n] = _all_sublanes(tau, jnp.minimum)[0:1]


def _route(x1, g_ffn, wq_t, keys, tt=256):
    t = x1.shape[0]
    return pl.pallas_call(
        _route_body,
        grid=(t // tt,),
        in_specs=[
            pl.BlockSpec((tt, D_MODEL), lambda i: (i, 0)),
            pl.BlockSpec((1, D_MODEL), lambda i: (0, 0)),
            pl.BlockSpec(wq_t.shape, lambda i: (0, 0)),
            pl.BlockSpec(keys.shape, lambda i: (0, 0, 0)),
        ],
        out_specs=[
            pl.BlockSpec((D_MODEL, tt), lambda i: (0, i)),
            pl.BlockSpec((PEER_HEADS, PEER_NKEYS, tt), lambda i: (0, 0, i)),
            pl.BlockSpec((PEER_HEADS, PEER_NKEYS, tt), lambda i: (0, 0, i)),
            pl.BlockSpec((PEER_HEADS, 1, tt), lambda i: (0, 0, i)),
        ],
        out_shape=[
            jax.ShapeDtypeStruct((D_MODEL, t), BF16),
            jax.ShapeDtypeStruct((PEER_HEADS, PEER_NKEYS, t), F32),
            jax.ShapeDtypeStruct((PEER_HEADS, PEER_NKEYS, t), F32),
            jax.ShapeDtypeStruct((PEER_HEADS, 1, t), F32),
        ],
        compiler_params=_cparams(("arbitrary",), 48),
        name="peer_route",
    )(x1, g_ffn, wq_t, keys)


def _experts_body(te, ht_ref, u0_ref, un_ref, vt_ref, a_ref, b_ref, tau_ref, x_ref, g_ref, o_ref,
                  acc, s_even, s_odd, act):
    e = pl.program_id(1)
    last = pl.num_programs(1) - 1
    nblk = te // PEER_NKEYS
    tt = ht_ref.shape[1]

    @pl.when(e == 0)
    def _():
        acc[...] = jnp.zeros(acc.shape, F32)
        s_even[...] = _dot(u0_ref[...], ht_ref[...])

    def step(s_cur, s_next):
        s_next[...] = _dot(un_ref[...], ht_ref[...])
        a_rows = [a_ref[n] for n in range(PEER_HEADS)]
        for ib in range(nblk):
            rows = slice(ib * PEER_NKEYS, (ib + 1) * PEER_NKEYS)
            for lb in range(tt // LANES):
                lanes = slice(lb * LANES, (lb + 1) * LANES)
                gate = None
                for n in range(PEER_HEADS):
                    p = a_rows[n][ib:ib + 1, lanes] * b_ref[n, :, lanes]
                    term = jnp.where(p >= tau_ref[n, :, lanes], p, 0.0)
                    gate = term if gate is None else gate + term
                s = s_cur[rows, lanes]
                act[rows, lanes] = (gate * (s * (1.0 + lax.erf(s)))).astype(BF16)
        acc[...] += _dot(vt_ref[...], act[...])

    @pl.when(e % 2 == 0)
    def _():
        step(s_even, s_odd)

    @pl.when(e % 2 == 1)
    def _():
        step(s_odd, s_even)

    @pl.when(e == last)
    def _():
        o_ref[...] = _rms(x_ref[...] + acc[...].T, g_ref[...])


def _experts(h2t, u_b, vt_b, a_f, b_f, tau, x1, g_final, tt=512, te=1024):
    t = x1.shape[0]
    n_e = PEER_EXPERTS // te
    return pl.pallas_call(
        functools.partial(_experts_body, te),
        grid=(t // tt, n_e),
        in_specs=[
            pl.BlockSpec((D_MODEL, tt), lambda i, e: (0, i)),
            pl.BlockSpec((te, D_MODEL), lambda i, e: (0, 0)),
            pl.BlockSpec((te, D_MODEL), lambda i, e: (jnp.minimum(e + 1, n_e - 1), 0)),
            pl.BlockSpec((D_MODEL, te), lambda i, e: (0, e)),
            pl.BlockSpec((PEER_HEADS, te // PEER_NKEYS, tt), lambda i, e: (0, e, i)),
            pl.BlockSpec((PEER_HEADS, PEER_NKEYS, tt), lambda i, e: (0, 0, i)),
            pl.BlockSpec((PEER_HEADS, 1, tt), lambda i, e: (0, 0, i)),
            pl.BlockSpec((tt, D_MODEL), lambda i, e: (i, 0)),
            pl.BlockSpec((1, D_MODEL), lambda i, e: (0, 0)),
        ],
        out_specs=pl.BlockSpec((tt, D_MODEL), lambda i, e: (i, 0)),
        out_shape=jax.ShapeDtypeStruct((t, D_MODEL), F32),
        scratch_shapes=[pltpu.VMEM((D_MODEL, tt), F32), pltpu.VMEM((te, tt), F32),
                        pltpu.VMEM((te, tt), F32), pltpu.VMEM((te, tt), BF16)],
        compiler_params=_cparams(("arbitrary", "arbitrary"), 56),
        name="peer_experts",
    )(h2t, u_b, u_b, vt_b, a_f, b_f, tau, x1, g_final)


def _lambda_init(layer_index):
    return 0.8 - 0.6 * math.exp(-0.3 * (layer_index - 1))


def _pad_lanes(v):
    return jnp.pad(v.reshape(1, -1), ((0, 0), (0, LANES - v.shape[-1])))


def _layer(x, layer, g_mix, w_in, lam_q1, lam_k1, lam_q2, lam_k2, g_subln, conv_w, conv_b, dt_bias,
           a_log, d_skip, g_ssd, w_branch_a, w_branch_b, w_out, g_ffn, w_query, sub_keys,
           expert_u, expert_v, g_out):
    b, s, d = x.shape
    t = b * s
    x2 = x.reshape(t, d)
    row = lambda v: v.reshape(1, -1)

    w = w_in[layer]
    dt_lo = COL_GATE
    q_scale = DA_HEAD_DIM ** -0.5 * LOG2E
    w_main = jnp.concatenate([w[:, :COL_K] * q_scale, w[:, COL_K:dt_lo], w[:, dt_lo + SSD_HEADS:]],
                             axis=1).astype(BF16)
    w_dt = jnp.pad(w[:, dt_lo:dt_lo + SSD_HEADS], ((0, 0), (0, LANES - SSD_HEADS))).astype(BF16)
    slopes = jnp.exp2(-8.0 * jnp.arange(1, DA_HEADS + 1, dtype=F32) / DA_HEADS)
    slopes = jnp.broadcast_to(slopes[:, None, None], (DA_HEADS, 1, LANES))

    proj, dt_raw = _inproj(x2, row(g_mix[layer]), w_main, w_dt)
    proj3 = proj.reshape(b, s, N_MAIN)

    att = _attention(proj3, slopes, row(lam_q1[layer]), row(lam_k1[layer]), row(lam_q2[layer]),
                     row(lam_k2[layer]), row(g_subln[layer]), _lambda_init(layer + 1))
    ssd = _ssd(proj3, dt_raw.reshape(b, s, LANES), conv_w[layer].reshape(SSD_CONV, SSD_XBC),
               row(conv_b[layer]), _pad_lanes(dt_bias[layer]), _pad_lanes(a_log[layer]),
               row(jnp.repeat(d_skip[layer], SSD_HEAD_DIM)), row(g_ssd[layer]))

    x1 = _merge(att.reshape(t, d), ssd.reshape(t, d), proj, x2, w_branch_a[layer].astype(BF16),
                w_branch_b[layer].astype(BF16), w_out[layer].astype(BF16))

    keys = sub_keys[layer].reshape(2 * PEER_HEADS, PEER_NKEYS, PEER_HALF).astype(BF16)
    h2t, a_f, b_f, tau = _route(x1, row(g_ffn[layer]), w_query[layer].T.astype(BF16), keys)
    out = _experts(h2t, (expert_u[layer] * GELU_ARG_SCALE).astype(BF16), expert_v[layer].T.astype(BF16), a_f, b_f, tau,
                   x1, row(g_out))
    return out.reshape(b, s, d)


def kernel(x, g_mix, w_in, lam_q1, lam_k1, lam_q2, lam_k2, g_subln, conv_w, conv_b, dt_bias, a_log,
           d_skip, g_ssd, w_branch_a, w_branch_b, w_out, g_ffn, w_query, sub_keys, expert_u, expert_v,
           g_final):
    depth = g_mix.shape[0]
    assert depth == 1, "the final RMSNorm is fused into the last (only) layer's expert kernel"
    return _layer(x, 0, g_mix, w_in, lam_q1, lam_k1, lam_q2, lam_k2, g_subln, conv_w, conv_b, dt_bias,
                  a_log, d_skip, g_ssd, w_branch_a, w_branch_b, w_out, g_ffn, w_query, sub_keys,
                  expert_u, expert_v, g_final)
```

```python
import functools
import math

import jax
import jax.numpy as jnp
from jax import lax
from jax.experimental import pallas as pl
from jax.experimental.pallas import tpu as pltpu

F32 = jnp.float32
BF16 = jnp.bfloat16

D_MODEL = 1024
CHUNK = 64
EPS = 1e-6
LOG2E = 1.4426950408889634

DA_HEADS = 8
DA_HEAD_DIM = 64
DA_V_DIM = 128

SSD_HEADS = 16
SSD_HEAD_DIM = 64
SSD_GROUPS = 4
SSD_STATE = 128
SSD_CONV = 4
SSD_XBC = 2048
GROUP_LANES = (SSD_HEADS // SSD_GROUPS) * SSD_HEAD_DIM

PEER_HEADS = 8
PEER_NKEYS = 128
PEER_EXPERTS = PEER_NKEYS * PEER_NKEYS
PEER_HALF = 128
PEER_TOPK = 16
GELU_ARG_SCALE = 1.0 / math.sqrt(2.0)
GELU_OUT_SCALE = 1.0 / math.sqrt(2.0)

N_MAIN = 8192
COL_Q, COL_K, COL_V, COL_Z, COL_XBC, COL_GATE = 0, 1024, 2048, 3072, 4096, 6144
LANES = 128

NEG_INF = float("-inf")


def _cparams(sem, vmem_mb):
    return pltpu.CompilerParams(dimension_semantics=sem, vmem_limit_bytes=vmem_mb * 1024 * 1024)


def _rms(x, g):
    return x * lax.rsqrt(jnp.mean(x * x, axis=-1, keepdims=True) + EPS) * g


def _dot(a, b):
    return jnp.dot(a, b, preferred_element_type=F32)


def _dot_nt(a, b):
    return lax.dot_general(a, b, (((1,), (1,)), ((), ())), preferred_element_type=F32)


def _dot_tn(a, b):
    return lax.dot_general(a, b, (((0,), (0,)), ((), ())), preferred_element_type=F32)


def _inproj_body(x_ref, g_ref, w_ref, wdt_ref, o_ref, dt_ref, h_scr):
    @pl.when(pl.program_id(1) == 0)
    def _():
        hb = _rms(x_ref[...], g_ref[...]).astype(BF16)
        h_scr[...] = hb
        dt_ref[...] = _dot(hb, wdt_ref[...])

    o_ref[...] = _dot(h_scr[...], w_ref[...]).astype(BF16)


def _inproj(x2, g_mix, w_main, w_dt, tm=1024, tn=1024):
    t = x2.shape[0]
    return pl.pallas_call(
        _inproj_body,
        grid=(t // tm, N_MAIN // tn),
        in_specs=[
            pl.BlockSpec((tm, D_MODEL), lambda i, j: (i, 0)),
            pl.BlockSpec((1, D_MODEL), lambda i, j: (0, 0)),
            pl.BlockSpec((D_MODEL, tn), lambda i, j: (0, j)),
            pl.BlockSpec((D_MODEL, LANES), lambda i, j: (0, 0)),
        ],
        out_specs=[
            pl.BlockSpec((tm, tn), lambda i, j: (i, j)),
            pl.BlockSpec((tm, LANES), lambda i, j: (i, 0)),
        ],
        out_shape=[
            jax.ShapeDtypeStruct((t, N_MAIN), BF16),
            jax.ShapeDtypeStruct((t, LANES), F32),
        ],
        scratch_shapes=[pltpu.VMEM((tm, D_MODEL), BF16)],
        compiler_params=_cparams(("arbitrary", "arbitrary"), 40),
        name="inproj",
    )(x2, g_mix, w_main, w_dt)


ATT_ROWS = DA_V_DIM + 16


def _attn_body(lam0, tq, slope_ref, lq1_ref, lk1_ref, lq2_ref, lk2_ref, gsub_ref,
               q_ref, k_ref, v_ref, o_ref,
               bdiag, ka, kb_, vt, acc1, acc2, ua, ub):
    qi = pl.program_id(2)
    slope2 = slope_ref[0][:, 0:1] * LOG2E
    n_kv = v_ref.shape[1] // tq

    @pl.when(qi == 0)
    def _():
        jj = lax.broadcasted_iota(jnp.int32, (tq, tq), 0)
        ii = lax.broadcasted_iota(jnp.int32, (tq, tq), 1)
        d = (ii - jj).astype(F32)
        visible = (jj // CHUNK) <= (ii // CHUNK)
        bdiag[...] = jnp.where(visible, -slope2 * (jnp.abs(d) - d), NEG_INF)

        lane = lax.broadcasted_iota(jnp.int32, (tq, LANES), 1)
        c = slope2 * lax.broadcasted_iota(jnp.int32, (tq, LANES), 0).astype(F32)
        c_hi = c.astype(BF16).astype(F32)
        c_mid = (c - c_hi).astype(BF16).astype(F32)
        c_lo = c - c_hi - c_mid

        def extras(base):
            return jnp.where(lane == base, c_hi, jnp.where(lane == base + 1, c_mid,
                             jnp.where(lane == base + 2, c_lo, 0.0)))

        ext_a, ext_b = extras(DA_HEAD_DIM), extras(0)
        ones_row = lax.broadcasted_iota(jnp.int32, (ATT_ROWS - DA_V_DIM, tq), 0) == 0
        for cblk in range(n_kv):
            rows = slice(cblk * tq, (cblk + 1) * tq)
            kblk = k_ref[0, rows, :].astype(F32)
            ka[rows, :] = jnp.where(lane < DA_HEAD_DIM, kblk, ext_a).astype(BF16)
            kb_[rows, :] = jnp.where(lane >= DA_HEAD_DIM, kblk, ext_b).astype(BF16)
            vt[0:DA_V_DIM, rows] = v_ref[0, rows, :].astype(F32).T.astype(BF16)
            vt[DA_V_DIM:, rows] = jnp.where(ones_row, 1.0, 0.0).astype(BF16)

    qt = q_ref[0].astype(F32).T
    row = lax.broadcasted_iota(jnp.int32, qt.shape, 0)
    qa = jnp.where(row < DA_HEAD_DIM, qt, jnp.where(row < DA_HEAD_DIM + 3, 1.0, 0.0)).astype(BF16)
    qb = jnp.where(row >= DA_HEAD_DIM, qt, jnp.where(row < 3, 1.0, 0.0)).astype(BF16)

    streams = ((ka, qa, acc1), (kb_, qb, acc2))

    def scores(start, rows, dst):
        for mi, (kref, qm, _) in enumerate(streams):
            dst[mi] = _dot(kref[pl.ds(start, rows), :], qm)

    def absorb(src, start, bias, shift, ms):
        v_t = vt[:, pl.ds(start, hk)]
        out = []
        for mi, ((_, _, acc), m_old) in enumerate(zip(streams, ms)):
            u = src[mi]
            if bias is not None:
                u = u + bias
            m_new = jnp.maximum(m_old, jnp.max(u, axis=0, keepdims=True) + shift)
            p = jnp.exp2(u - (m_new - shift))
            pv = _dot(v_t, p.astype(BF16))
            acc[...] = jnp.exp2(m_old - m_new) * acc[...] + pv
            out.append(m_new)
        return tuple(out)

    hk = tq // 2
    acc1[...] = jnp.zeros(acc1.shape, F32)
    acc2[...] = jnp.zeros(acc2.shape, F32)
    neg = jnp.full((1, tq), NEG_INF, F32)
    scores(0, hk, ua)

    def body(j, ms):
        s0 = pl.multiple_of(2 * j * hk, hk)
        shift = -slope2 * ((qi - j) * tq).astype(F32)
        scores(s0 + hk, hk, ub)
        ms = absorb(ua, s0, None, shift, ms)
        scores(s0 + 2 * hk, hk, ua)
        return absorb(ub, s0 + hk, None, shift, ms)

    ms = lax.fori_loop(0, qi, body, (neg, neg))
    diag = pl.multiple_of(qi * tq, tq)
    no_shift = jnp.zeros((1, 1), F32)
    scores(diag + hk, hk, ub)
    ms = absorb(ua, diag, bdiag[0:hk, :], no_shift, ms)
    absorb(ub, diag + hk, bdiag[hk:tq, :], no_shift, ms)

    lam = (jnp.exp(jnp.sum(lq1_ref[...] * lk1_ref[...], axis=-1, keepdims=True))
           - jnp.exp(jnp.sum(lq2_ref[...] * lk2_ref[...], axis=-1, keepdims=True)) + lam0)
    o = (acc1[0:DA_V_DIM, :] / acc1[DA_V_DIM:DA_V_DIM + 1, :]
         - lam * (acc2[0:DA_V_DIM, :] / acc2[DA_V_DIM:DA_V_DIM + 1, :]))
    o = o * lax.rsqrt(jnp.mean(o * o, axis=0, keepdims=True) + EPS)
    o_ref[0] = (o.T * (gsub_ref[...] * (1.0 - lam0))).astype(BF16)


def _attention(proj3, slopes, lq1, lk1, lq2, lk2, g_subln, lam0, tq=512):
    b, s, _ = proj3.shape
    kcol, vcol = COL_K // LANES, COL_V // LANES
    small = lambda shape: pl.BlockSpec(shape, lambda bi, h, qi: (0,) * len(shape))
    return pl.pallas_call(
        functools.partial(_attn_body, lam0, tq),
        grid=(b, DA_HEADS, s // tq),
        in_specs=[
            pl.BlockSpec((1, 1, LANES), lambda bi, h, qi: (h, 0, 0)),
            small((1, DA_HEAD_DIM)), small((1, DA_HEAD_DIM)),
            small((1, DA_HEAD_DIM)), small((1, DA_HEAD_DIM)),
            small((1, DA_V_DIM)),
            pl.BlockSpec((1, tq, LANES), lambda bi, h, qi: (bi, qi, h)),
            pl.BlockSpec((1, s, LANES), lambda bi, h, qi: (bi, 0, kcol + h)),
            pl.BlockSpec((1, s, LANES), lambda bi, h, qi: (bi, 0, vcol + h)),
        ],
        out_specs=pl.BlockSpec((1, tq, LANES), lambda bi, h, qi: (bi, qi, h)),
        out_shape=jax.ShapeDtypeStruct((b, s, DA_HEADS * DA_V_DIM), BF16),
        scratch_shapes=[
            pltpu.VMEM((tq, tq), F32),
            pltpu.VMEM((s, LANES), BF16), pltpu.VMEM((s, LANES), BF16),
            pltpu.VMEM((ATT_ROWS, s), BF16),
            pltpu.VMEM((ATT_ROWS, tq), F32), pltpu.VMEM((ATT_ROWS, tq), F32),
            pltpu.VMEM((2, tq // 2, tq), F32), pltpu.VMEM((2, tq // 2, tq), F32),
        ],
        compiler_params=_cparams(("arbitrary", "arbitrary", "arbitrary"), 40),
        name="diff_attention",
    )(slopes, lq1, lk1, lq2, lk2, g_subln, proj3, proj3, proj3)


def _split3(x):
    hi = x.astype(BF16)
    r = x - hi.astype(F32)
    mid = r.astype(BF16)
    lo = (r - mid.astype(F32)).astype(BF16)
    return hi, mid, lo


def _dot3_rhs(a_b, x):
    hi, mid, lo = _split3(x)
    return _dot(a_b, hi) + _dot(a_b, mid) + _dot(a_b, lo)


def _dot3_lhs(x, a_b):
    hi, mid, lo = _split3(x)
    return _dot(hi, a_b) + _dot(mid, a_b) + _dot(lo, a_b)


def _silu(x):
    return x * jax.nn.sigmoid(x)


def _ssd_body(ts, z_ref, xbc_ref, dt_ref, cw_ref, cb_ref, dtb_ref, alog_ref, dexp_ref, g_ref,
              tril_ref, ones_ref, expand_ref, eye_ref, trilrep_ref, bd_ref,
              o_ref, xe, state):
    ti = pl.program_id(1)

    @pl.when(ti == 0)
    def _():
        xe[0:8, :] = jnp.zeros((8, SSD_XBC), F32)
        state[...] = jnp.zeros(state.shape, F32)

    xe[8:8 + ts, :] = xbc_ref[0].astype(F32)
    conv = cb_ref[...] + cw_ref[0:1, :] * xe[5:5 + ts, :]
    for kk in range(1, SSD_CONV):
        conv = conv + cw_ref[kk:kk + 1, :] * xe[5 + kk:5 + kk + ts, :]
    tail = xe[ts:ts + 8, :]
    xe[0:8, :] = tail
    xc = _silu(conv)
    xs = xc[:, :D_MODEL]
    bm = xc[:, D_MODEL:D_MODEL + SSD_GROUPS * SSD_STATE].astype(BF16)
    cm = xc[:, D_MODEL + SSD_GROUPS * SSD_STATE:].astype(BF16)

    dt = jax.nn.softplus(dt_ref[0] + dtb_ref[...])
    a = -jnp.exp(alog_ref[...]) * dt
    acum = _dot3_rhs(tril_ref[...], a)
    colexp = _dot3_lhs(acum, expand_ref[...])
    dtexp = _dot3_lhs(dt, expand_ref[...])
    rowexp = _dot3_rhs(ones_ref[...], colexp * eye_ref[...])
    lmat = jnp.exp(jnp.where(trilrep_ref[...] > 0.5, colexp - rowexp, NEG_INF))
    xdt = xs * dtexp
    decay_in = jnp.exp(colexp)

    bdmask = bd_ref[...] > 0.5
    ys = []
    for c in range(ts // CHUNK):
        rows = slice(c * CHUNK, (c + 1) * CHUNK)
        a_last = colexp[(c + 1) * CHUNK - 1:(c + 1) * CHUNK, :]
        decay_st = jnp.exp(a_last - colexp[rows, :])
        chunk_decay = jnp.exp(a_last)
        xd = (xdt[rows, :] * decay_st).astype(BF16)
        xb = xdt[rows, :].astype(BF16)
        yg = []
        for g in range(SSD_GROUPS):
            lanes = slice(g * GROUP_LANES, (g + 1) * GROUP_LANES)
            cg = cm[rows, g * SSD_STATE:(g + 1) * SSD_STATE]
            bg = bm[rows, g * SSD_STATE:(g + 1) * SSD_STATE]
            cb = _dot_nt(cg, jnp.concatenate([bg] * 4, axis=0))
            w = (cb * lmat[rows, lanes]).astype(BF16)
            xg = xb[:, lanes]
            xbd = jnp.where(bdmask, jnp.concatenate([xg] * 4, axis=0), jnp.zeros((), BF16))
            y_diag = _dot(w, xbd)
            prev = state[g]
            y_off = _dot(cg, prev.astype(BF16)) * decay_in[rows, lanes]
            state[g] = prev * chunk_decay[:, lanes] + _dot_tn(bg, xd[:, lanes])
            yg.append(y_diag + y_off)
        ys.append(jnp.concatenate(yg, axis=1))
    y = jnp.concatenate(ys, axis=0) + dexp_ref[...] * xs
    y = y * _silu(z_ref[0].astype(F32))
    o_ref[0] = _rms(y, g_ref[...]).astype(BF16)


def _ssd_constants(ts):
    r = jnp.arange(ts)
    same_chunk = (r[:, None] // CHUNK) == (r[None, :] // CHUNK)
    tril = (same_chunk & (r[:, None] >= r[None, :])).astype(BF16)
    ones = same_chunk.astype(BF16)
    lane = jnp.arange(D_MODEL)
    expand = (jnp.arange(LANES)[:, None] == (lane[None, :] // SSD_HEAD_DIM)).astype(BF16)
    l_in = r[:, None] % CHUNK
    s_in = lane[None, :] % SSD_HEAD_DIM
    eye = (l_in == s_in).astype(F32)
    trilrep = (l_in >= s_in).astype(F32)
    q = jnp.arange(GROUP_LANES)
    bd = ((q[:, None] // SSD_HEAD_DIM) == (q[None, :] // SSD_HEAD_DIM)).astype(F32)
    return tril, ones, expand, eye, trilrep, bd


def _ssd(proj3, dt3, conv_w, conv_b, dt_bias, a_log, d_exp, g_ssd, ts=256):
    b, s, _ = proj3.shape
    consts = _ssd_constants(ts)
    full = lambda arr: pl.BlockSpec(arr.shape, lambda bi, ti: (0,) * arr.ndim)
    params = (conv_w, conv_b, dt_bias, a_log, d_exp, g_ssd)
    return pl.pallas_call(
        functools.partial(_ssd_body, ts),
        grid=(b, s // ts),
        in_specs=[
            pl.BlockSpec((1, ts, D_MODEL), lambda bi, ti: (bi, ti, COL_Z // D_MODEL)),
            pl.BlockSpec((1, ts, SSD_XBC), lambda bi, ti: (bi, ti, COL_XBC // SSD_XBC)),
            pl.BlockSpec((1, ts, LANES), lambda bi, ti: (bi, ti, 0)),
        ] + [full(p) for p in params] + [full(c) for c in consts],
        out_specs=pl.BlockSpec((1, ts, D_MODEL), lambda bi, ti: (bi, ti, 0)),
        out_shape=jax.ShapeDtypeStruct((b, s, D_MODEL), BF16),
        scratch_shapes=[
            pltpu.VMEM((ts + 8, SSD_XBC), F32),
            pltpu.VMEM((SSD_GROUPS, SSD_STATE, GROUP_LANES), F32),
        ],
        compiler_params=_cparams(("arbitrary", "arbitrary"), 48),
        name="ssd_mixer",
    )(proj3, proj3, dt3, *params, *consts)


def _merge_body(att_ref, ssd_ref, ga_ref, gb_ref, x_ref, wa_ref, wb_ref, wo_ref, o_ref):
    ya = _dot(att_ref[...], wa_ref[...])
    yb = _dot(ssd_ref[...], wb_ref[...])
    ga = jax.nn.sigmoid(ga_ref[...].astype(F32))
    gb = jax.nn.sigmoid(gb_ref[...].astype(F32))
    m = (ga * ya + gb * yb).astype(BF16)
    o_ref[...] = x_ref[...] + _dot(m, wo_ref[...])


def _merge(att2, ssd2, proj, x2, wa, wb, wo, tm=512):
    t = x2.shape[0]
    row = lambda c: pl.BlockSpec((tm, D_MODEL), lambda i: (i, c))
    wspec = pl.BlockSpec((D_MODEL, D_MODEL), lambda i: (0, 0))
    gcol = COL_GATE // D_MODEL
    return pl.pallas_call(
        _merge_body,
        grid=(t // tm,),
        in_specs=[row(0), row(0), row(gcol), row(gcol + 1), row(0), wspec, wspec, wspec],
        out_specs=row(0),
        out_shape=jax.ShapeDtypeStruct((t, D_MODEL), F32),
        compiler_params=_cparams(("arbitrary",), 40),
        name="gated_merge",
    )(att2, ssd2, proj, proj, x2, wa, wb, wo)


SUBLANES = 8


def _batcher_network(n):
    def merge(lo, hi, r):
        step = r * 2
        if step < hi - lo:
            yield from merge(lo, hi, step)
            yield from merge(lo + r, hi, step)
            yield from [(i, i + r) for i in range(lo + r, hi - r, step)]
        else:
            yield (lo, lo + r)

    def sort(lo, hi):
        if hi - lo >= 1:
            mid = lo + (hi - lo) // 2
            yield from sort(lo, mid)
            yield from sort(mid + 1, hi)
            yield from merge(lo, hi, 1)

    return tuple(sort(0, n - 1))


_SORT16 = _batcher_network(PEER_TOPK)


def _exchange(xs, i, j):
    xs[i], xs[j] = jnp.maximum(xs[i], xs[j]), jnp.minimum(xs[i], xs[j])


def _all_sublanes(x, op):
    for shift in (4, 2, 1):
        x = op(x, pltpu.roll(x, shift, 0))
    return x


def _top16(s):
    xs = [s[SUBLANES * v:SUBLANES * (v + 1), :] for v in range(PEER_TOPK)]
    for i, j in _SORT16:
        _exchange(xs, i, j)
    for shift in (4, 2, 1):
        ys = [pltpu.roll(x, shift, 0) for x in xs]
        xs = [jnp.maximum(xs[k], ys[PEER_TOPK - 1 - k]) for k in range(PEER_TOPK)]
        for d in (8, 4, 2, 1):
            for i in range(PEER_TOPK):
                if i & d == 0:
                    _exchange(xs, i, i + d)
    return xs


def _stack(rows, sub):
    out = rows[SUBLANES - 1]
    for r in range(SUBLANES - 2, -1, -1):
        out = jnp.where(sub == r, rows[r], out)
    return out


def _candidate_groups(a_rep, a_hi, b_lo, b_hi, b0, combine):
    groups = [combine(a_rep[0], b_lo), combine(a_rep[0], b_hi)]
    groups += [combine(a_rep[a], b_lo) for a in range(1, SUBLANES)]
    groups.append(combine(a_hi, b0))
    return groups


def _route_body(x_ref, g_ref, wq_ref, keys_ref, ht_ref, a_ref, b_ref, tau_ref):
    hb = _rms(x_ref[...], g_ref[...])
    ht = hb.T.astype(BF16)
    ht_ref[...] = ht
    qt = _dot(wq_ref[...], ht).astype(BF16)
    tt = ht.shape[1]
    sub = lax.broadcasted_iota(jnp.int32, (SUBLANES, tt), 0)
    for n in range(PEER_HEADS):
        s1 = _dot(keys_ref[2 * n], qt[(2 * n) * PEER_HALF:(2 * n + 1) * PEER_HALF, :])
        s2 = _dot(keys_ref[2 * n + 1], qt[(2 * n + 1) * PEER_HALF:(2 * n + 2) * PEER_HALF, :])
        r1 = _top16(s1)
        r2 = _top16(s2)
        add = lambda p, q: p + q
        cand = _candidate_groups(r1, _stack(r1[SUBLANES:], sub), _stack(r2[:SUBLANES], sub),
                                 _stack(r2[SUBLANES:], sub), r2[0], add)
        work = cand
        for it in range(PEER_TOPK):
            m = functools.reduce(jnp.maximum, work)
            m = _all_sublanes(m, jnp.maximum)
            if it < PEER_TOPK - 1:
                work = [jnp.where(w == m, NEG_INF, w) for w in work]
        sel = [c >= m for c in cand]
        top = r1[0] + r2[0]
        z = functools.reduce(add, [jnp.where(s, jnp.exp(c - top), 0.0) for s, c in zip(sel, cand)])
        inv_z = GELU_OUT_SCALE / _all_sublanes(z, add)
        a_ref[n] = jnp.exp(s1 - r1[0][0:1]) * inv_z[0:1]
        b_ref[n] = jnp.exp(s2 - r2[0][0:1])
        pa = [jnp.exp(r - r1[0]) * inv_z for r in r1]
        pb = [jnp.exp(r - r2[0]) for r in r2]
        prod = _candidate_groups(pa, _stack(pa[SUBLANES:], sub), _stack(pb[:SUBLANES], sub),
                                 _stack(pb[SUBLANES:], sub), pb[0], lambda p, q: p * q)
        tau = functools.reduce(jnp.minimum, [jnp.where(s, p, jnp.inf) for s, p in zip(sel, prod)])
        tau_ref[n] = _all_sublanes(tau, jnp.minimum)[0:1]


def _route(x1, g_ffn, wq_t, keys, tt=256):
    t = x1.shape[0]
    return pl.pallas_call(
        _route_body,
        grid=(t // tt,),
        in_specs=[
            pl.BlockSpec((tt, D_MODEL), lambda i: (i, 0)),
            pl.BlockSpec((1, D_MODEL), lambda i: (0, 0)),
            pl.BlockSpec(wq_t.shape, lambda i: (0, 0)),
            pl.BlockSpec(keys.shape, lambda i: (0, 0, 0)),
        ],
        out_specs=[
            pl.BlockSpec((D_MODEL, tt), lambda i: (0, i)),
            pl.BlockSpec((PEER_HEADS, PEER_NKEYS, tt), lambda i: (0, 0, i)),
            pl.BlockSpec((PEER_HEADS, PEER_NKEYS, tt), lambda i: (0, 0, i)),
            pl.BlockSpec((PEER_HEADS, 1, tt), lambda i: (0, 0, i)),
        ],
        out_shape=[
            jax.ShapeDtypeStruct((D_MODEL, t), BF16),
            jax.ShapeDtypeStruct((PEER_HEADS, PEER_NKEYS, t), F32),
            jax.ShapeDtypeStruct((PEER_HEADS, PEER_NKEYS, t), F32),
            jax.ShapeDtypeStruct((PEER_HEADS, 1, t), F32),
        ],
        compiler_params=_cparams(("arbitrary",), 48),
        name="peer_route",
    )(x1, g_ffn, wq_t, keys)


def _experts_body(te, ht_ref, u_ref, vt_ref, a_ref, b_ref, tau_ref, x_ref, g_ref, o_ref, acc, gates, act):
    e = pl.program_id(1)
    nblk = te // PEER_NKEYS
    tt = ht_ref.shape[1]

    @pl.when(e == 0)
    def _():
        acc[...] = jnp.zeros(acc.shape, F32)

    a_rows = [a_ref[n] for n in range(PEER_HEADS)]
    for ib in range(nblk):
        rows = slice(ib * PEER_NKEYS, (ib + 1) * PEER_NKEYS)
        for lb in range(tt // LANES):
            lanes = slice(lb * LANES, (lb + 1) * LANES)
            gate = None
            for n in range(PEER_HEADS):
                p = a_rows[n][ib:ib + 1, lanes] * b_ref[n, :, lanes]
                term = jnp.where(p >= tau_ref[n, :, lanes], p, 0.0)
                gate = term if gate is None else gate + term
            gates[rows, lanes] = gate
    s = _dot(u_ref[...], ht_ref[...])
    act[...] = (gates[...] * (s * (1.0 + lax.erf(s)))).astype(BF16)
    acc[...] += _dot(vt_ref[...], act[...])

    @pl.when(e == pl.num_programs(1) - 1)
    def _():
        o_ref[...] = _rms(x_ref[...] + acc[...].T, g_ref[...])


def _experts(h2t, u_b, vt_b, a_f, b_f, tau, x1, g_final, tt=512, te=1024):
    t = x1.shape[0]
    n_e = PEER_EXPERTS // te
    return pl.pallas_call(
        functools.partial(_experts_body, te),
        grid=(t // tt, n_e),
        in_specs=[
            pl.BlockSpec((D_MODEL, tt), lambda i, e: (0, i)),
            pl.BlockSpec((te, D_MODEL), lambda i, e: (e, 0)),
            pl.BlockSpec((D_MODEL, te), lambda i, e: (0, e)),
            pl.BlockSpec((PEER_HEADS, te // PEER_NKEYS, tt), lambda i, e: (0, e, i)),
            pl.BlockSpec((PEER_HEADS, PEER_NKEYS, tt), lambda i, e: (0, 0, i)),
            pl.BlockSpec((PEER_HEADS, 1, tt), lambda i, e: (0, 0, i)),
            pl.BlockSpec((tt, D_MODEL), lambda i, e: (i, 0)),
            pl.BlockSpec((1, D_MODEL), lambda i, e: (0, 0)),
        ],
        out_specs=pl.BlockSpec((tt, D_MODEL), lambda i, e: (i, 0)),
        out_shape=jax.ShapeDtypeStruct((t, D_MODEL), F32),
        scratch_shapes=[pltpu.VMEM((D_MODEL, tt), F32), pltpu.VMEM((te, tt), F32),
                        pltpu.VMEM((te, tt), BF16)],
        compiler_params=_cparams(("arbitrary", "arbitrary"), 56),
        name="peer_experts",
    )(h2t, u_b, vt_b, a_f, b_f, tau, x1, g_final)


def _lambda_init(layer_index):
    return 0.8 - 0.6 * math.exp(-0.3 * (layer_index - 1))


def _pad_lanes(v):
    return jnp.pad(v.reshape(1, -1), ((0, 0), (0, LANES - v.shape[-1])))


def _layer(x, layer, g_mix, w_in, lam_q1, lam_k1, lam_q2, lam_k2, g_subln, conv_w, conv_b, dt_bias,
           a_log, d_skip, g_ssd, w_branch_a, w_branch_b, w_out, g_ffn, w_query, sub_keys,
           expert_u, expert_v, g_out):
    b, s, d = x.shape
    t = b * s
    x2 = x.reshape(t, d)
    row = lambda v: v.reshape(1, -1)

    w = w_in[layer]
    dt_lo = COL_GATE
    q_scale = DA_HEAD_DIM ** -0.5 * LOG2E
    w_main = jnp.concatenate([w[:, :COL_K] * q_scale, w[:, COL_K:dt_lo], w[:, dt_lo + SSD_HEADS:]],
                             axis=1).astype(BF16)
    w_dt = jnp.pad(w[:, dt_lo:dt_lo + SSD_HEADS], ((0, 0), (0, LANES - SSD_HEADS))).astype(BF16)
    slopes = jnp.exp2(-8.0 * jnp.arange(1, DA_HEADS + 1, dtype=F32) / DA_HEADS)
    slopes = jnp.broadcast_to(slopes[:, None, None], (DA_HEADS, 1, LANES))

    proj, dt_raw = _inproj(x2, row(g_mix[layer]), w_main, w_dt)
    proj3 = proj.reshape(b, s, N_MAIN)

    att = _attention(proj3, slopes, row(lam_q1[layer]), row(lam_k1[layer]), row(lam_q2[layer]),
                     row(lam_k2[layer]), row(g_subln[layer]), _lambda_init(layer + 1))
    ssd = _ssd(proj3, dt_raw.reshape(b, s, LANES), conv_w[layer].reshape(SSD_CONV, SSD_XBC),
               row(conv_b[layer]), _pad_lanes(dt_bias[layer]), _pad_lanes(a_log[layer]),
               row(jnp.repeat(d_skip[layer], SSD_HEAD_DIM)), row(g_ssd[layer]))

    x1 = _merge(att.reshape(t, d), ssd.reshape(t, d), proj, x2, w_branch_a[layer].astype(BF16),
                w_branch_b[layer].astype(BF16), w_out[layer].astype(BF16))

    keys = sub_keys[layer].reshape(2 * PEER_HEADS, PEER_NKEYS, PEER_HALF).astype(BF16)
    h2t, a_f, b_f, tau = _route(x1, row(g_ffn[layer]), w_query[layer].T.astype(BF16), keys)
    out = _experts(h2t, (expert_u[layer] * GELU_ARG_SCALE).astype(BF16), expert_v[layer].T.astype(BF16), a_f, b_f, tau,
                   x1, row(g_out))
    return out.reshape(b, s, d)


def kernel(x, g_mix, w_in, lam_q1, lam_k1, lam_q2, lam_k2, g_subln, conv_w, conv_b, dt_bias, a_log,
           d_skip, g_ssd, w_branch_a, w_branch_b, w_out, g_ffn, w_query, sub_keys, expert_u, expert_v,
           g_final):
    depth = g_mix.shape[0]
    assert depth == 1, "the final RMSNorm is fused into the last (only) layer's expert kernel"
    return _layer(x, 0, g_mix, w_in, lam_q1, lam_k1, lam_q2, lam_k2, g_subln, conv_w, conv_b, dt_bias,
                  a_log, d_skip, g_ssd, w_branch_a, w_branch_b, w_out, g_ffn, w_query, sub_keys,
                  expert_u, expert_v, g_final)
```

```python
import functools
import math

import jax
import jax.numpy as jnp
from jax import lax
from jax.experimental import pallas as pl
from jax.experimental.pallas import tpu as pltpu

F32 = jnp.float32
BF16 = jnp.bfloat16

D_MODEL = 1024
CHUNK = 64
EPS = 1e-6
LOG2E = 1.4426950408889634

DA_HEADS = 8
DA_HEAD_DIM = 64
DA_V_DIM = 128

SSD_HEADS = 16
SSD_HEAD_DIM = 64
SSD_GROUPS = 4
SSD_STATE = 128
SSD_CONV = 4
SSD_XBC = 2048
GROUP_LANES = (SSD_HEADS // SSD_GROUPS) * SSD_HEAD_DIM

PEER_HEADS = 8
PEER_NKEYS = 128
PEER_EXPERTS = PEER_NKEYS * PEER_NKEYS
PEER_HALF = 128
PEER_TOPK = 16
GELU_ARG_SCALE = 1.0 / math.sqrt(2.0)
GELU_OUT_SCALE = 1.0 / math.sqrt(2.0)

N_MAIN = 8192
COL_Q, COL_K, COL_V, COL_Z, COL_XBC, COL_GATE = 0, 1024, 2048, 3072, 4096, 6144
LANES = 128

NEG_INF = float("-inf")


def _cparams(sem, vmem_mb):
    return pltpu.CompilerParams(dimension_semantics=sem, vmem_limit_bytes=vmem_mb * 1024 * 1024)


def _rms(x, g):
    return x * lax.rsqrt(jnp.mean(x * x, axis=-1, keepdims=True) + EPS) * g


def _dot(a, b):
    return jnp.dot(a, b, preferred_element_type=F32)


def _dot_nt(a, b):
    return lax.dot_general(a, b, (((1,), (1,)), ((), ())), preferred_element_type=F32)


def _dot_tn(a, b):
    return lax.dot_general(a, b, (((0,), (0,)), ((), ())), preferred_element_type=F32)


def _inproj_body(x_ref, g_ref, w_ref, wdt_ref, o_ref, dt_ref, h_scr):
    @pl.when(pl.program_id(1) == 0)
    def _():
        hb = _rms(x_ref[...], g_ref[...]).astype(BF16)
        h_scr[...] = hb
        dt_ref[...] = _dot(hb, wdt_ref[...])

    o_ref[...] = _dot(h_scr[...], w_ref[...]).astype(BF16)


def _inproj(x2, g_mix, w_main, w_dt, tm=1024, tn=1024):
    t = x2.shape[0]
    return pl.pallas_call(
        _inproj_body,
        grid=(t // tm, N_MAIN // tn),
        in_specs=[
            pl.BlockSpec((tm, D_MODEL), lambda i, j: (i, 0)),
            pl.BlockSpec((1, D_MODEL), lambda i, j: (0, 0)),
            pl.BlockSpec((D_MODEL, tn), lambda i, j: (0, j)),
            pl.BlockSpec((D_MODEL, LANES), lambda i, j: (0, 0)),
        ],
        out_specs=[
            pl.BlockSpec((tm, tn), lambda i, j: (i, j)),
            pl.BlockSpec((tm, LANES), lambda i, j: (i, 0)),
        ],
        out_shape=[
            jax.ShapeDtypeStruct((t, N_MAIN), BF16),
            jax.ShapeDtypeStruct((t, LANES), F32),
        ],
        scratch_shapes=[pltpu.VMEM((tm, D_MODEL), BF16)],
        compiler_params=_cparams(("arbitrary", "arbitrary"), 40),
        name="inproj",
    )(x2, g_mix, w_main, w_dt)


ATT_ROWS = DA_V_DIM + 16


def _attn_body(lam0, tq, slope_ref, lq1_ref, lk1_ref, lq2_ref, lk2_ref, gsub_ref,
               q_ref, k_ref, v_ref, o_ref,
               bdiag, ka, kb_, vt, acc1, acc2, ua, ub):
    qi = pl.program_id(2)
    slope2 = slope_ref[0][:, 0:1] * LOG2E
    n_kv = v_ref.shape[1] // tq

    @pl.when(qi == 0)
    def _():
        jj = lax.broadcasted_iota(jnp.int32, (tq, tq), 0)
        ii = lax.broadcasted_iota(jnp.int32, (tq, tq), 1)
        d = (ii - jj).astype(F32)
        visible = (jj // CHUNK) <= (ii // CHUNK)
        bdiag[...] = jnp.where(visible, -slope2 * (jnp.abs(d) - d), NEG_INF)

        lane = lax.broadcasted_iota(jnp.int32, (tq, LANES), 1)
        c = slope2 * lax.broadcasted_iota(jnp.int32, (tq, LANES), 0).astype(F32)
        c_hi = c.astype(BF16).astype(F32)
        c_mid = (c - c_hi).astype(BF16).astype(F32)
        c_lo = c - c_hi - c_mid

        def extras(base):
            return jnp.where(lane == base, c_hi, jnp.where(lane == base + 1, c_mid,
                             jnp.where(lane == base + 2, c_lo, 0.0)))

        ext_a, ext_b = extras(DA_HEAD_DIM), extras(0)
        ones_row = lax.broadcasted_iota(jnp.int32, (ATT_ROWS - DA_V_DIM, tq), 0) == 0
        for cblk in range(n_kv):
            rows = slice(cblk * tq, (cblk + 1) * tq)
            kblk = k_ref[0, rows, :].astype(F32)
            ka[rows, :] = jnp.where(lane < DA_HEAD_DIM, kblk, ext_a).astype(BF16)
            kb_[rows, :] = jnp.where(lane >= DA_HEAD_DIM, kblk, ext_b).astype(BF16)
            vt[0:DA_V_DIM, rows] = v_ref[0, rows, :].astype(F32).T.astype(BF16)
            vt[DA_V_DIM:, rows] = jnp.where(ones_row, 1.0, 0.0).astype(BF16)

    qt = q_ref[0].astype(F32).T
    row = lax.broadcasted_iota(jnp.int32, qt.shape, 0)
    qa = jnp.where(row < DA_HEAD_DIM, qt, jnp.where(row < DA_HEAD_DIM + 3, 1.0, 0.0)).astype(BF16)
    qb = jnp.where(row >= DA_HEAD_DIM, qt, jnp.where(row < 3, 1.0, 0.0)).astype(BF16)

    streams = ((ka, qa, acc1), (kb_, qb, acc2))

    def scores(start, rows, dst):
        for mi, (kref, qm, _) in enumerate(streams):
            dst[mi] = _dot(kref[pl.ds(start, rows), :], qm)

    def absorb(src, start, bias, shift, ms):
        v_t = vt[:, pl.ds(start, hk)]
        out = []
        for mi, ((_, _, acc), m_old) in enumerate(zip(streams, ms)):
            u = src[mi]
            if bias is not None:
                u = u + bias
            m_new = jnp.maximum(m_old, jnp.max(u, axis=0, keepdims=True) + shift)
            p = jnp.exp2(u - (m_new - shift))
            pv = _dot(v_t, p.astype(BF16))
            acc[...] = jnp.exp2(m_old - m_new) * acc[...] + pv
            out.append(m_new)
        return tuple(out)

    hk = tq // 2
    acc1[...] = jnp.zeros(acc1.shape, F32)
    acc2[...] = jnp.zeros(acc2.shape, F32)
    neg = jnp.full((1, tq), NEG_INF, F32)
    scores(0, hk, ua)

    def body(j, ms):
        s0 = pl.multiple_of(2 * j * hk, hk)
        shift = -slope2 * ((qi - j) * tq).astype(F32)
        scores(s0 + hk, hk, ub)
        ms = absorb(ua, s0, None, shift, ms)
        scores(s0 + 2 * hk, hk, ua)
        return absorb(ub, s0 + hk, None, shift, ms)

    ms = lax.fori_loop(0, qi, body, (neg, neg))
    diag = pl.multiple_of(qi * tq, tq)
    no_shift = jnp.zeros((1, 1), F32)
    scores(diag + hk, hk, ub)
    ms = absorb(ua, diag, bdiag[0:hk, :], no_shift, ms)
    absorb(ub, diag + hk, bdiag[hk:tq, :], no_shift, ms)

    lam = (jnp.exp(jnp.sum(lq1_ref[...] * lk1_ref[...], axis=-1, keepdims=True))
           - jnp.exp(jnp.sum(lq2_ref[...] * lk2_ref[...], axis=-1, keepdims=True)) + lam0)
    o = (acc1[0:DA_V_DIM, :] / acc1[DA_V_DIM:DA_V_DIM + 1, :]
         - lam * (acc2[0:DA_V_DIM, :] / acc2[DA_V_DIM:DA_V_DIM + 1, :]))
    o = o * lax.rsqrt(jnp.mean(o * o, axis=0, keepdims=True) + EPS)
    o_ref[0] = (o.T * (gsub_ref[...] * (1.0 - lam0))).astype(BF16)


def _attention(proj3, slopes, lq1, lk1, lq2, lk2, g_subln, lam0, tq=1024):
    b, s, _ = proj3.shape
    kcol, vcol = COL_K // LANES, COL_V // LANES
    small = lambda shape: pl.BlockSpec(shape, lambda bi, h, qi: (0,) * len(shape))
    return pl.pallas_call(
        functools.partial(_attn_body, lam0, tq),
        grid=(b, DA_HEADS, s // tq),
        in_specs=[
            pl.BlockSpec((1, 1, LANES), lambda bi, h, qi: (h, 0, 0)),
            small((1, DA_HEAD_DIM)), small((1, DA_HEAD_DIM)),
            small((1, DA_HEAD_DIM)), small((1, DA_HEAD_DIM)),
            small((1, DA_V_DIM)),
            pl.BlockSpec((1, tq, LANES), lambda bi, h, qi: (bi, qi, h)),
            pl.BlockSpec((1, s, LANES), lambda bi, h, qi: (bi, 0, kcol + h)),
            pl.BlockSpec((1, s, LANES), lambda bi, h, qi: (bi, 0, vcol + h)),
        ],
        out_specs=pl.BlockSpec((1, tq, LANES), lambda bi, h, qi: (bi, qi, h)),
        out_shape=jax.ShapeDtypeStruct((b, s, DA_HEADS * DA_V_DIM), BF16),
        scratch_shapes=[
            pltpu.VMEM((tq, tq), F32),
            pltpu.VMEM((s, LANES), BF16), pltpu.VMEM((s, LANES), BF16),
            pltpu.VMEM((ATT_ROWS, s), BF16),
            pltpu.VMEM((ATT_ROWS, tq), F32), pltpu.VMEM((ATT_ROWS, tq), F32),
            pltpu.VMEM((2, tq // 2, tq), F32), pltpu.VMEM((2, tq // 2, tq), F32),
        ],
        compiler_params=_cparams(("arbitrary", "arbitrary", "arbitrary"), 40),
        name="diff_attention",
    )(slopes, lq1, lk1, lq2, lk2, g_subln, proj3, proj3, proj3)


def _split3(x):
    hi = x.astype(BF16)
    r = x - hi.astype(F32)
    mid = r.astype(BF16)
    lo = (r - mid.astype(F32)).astype(BF16)
    return hi, mid, lo


def _dot3_rhs(a_b, x):
    hi, mid, lo = _split3(x)
    return _dot(a_b, hi) + _dot(a_b, mid) + _dot(a_b, lo)


def _dot3_lhs(x, a_b):
    hi, mid, lo = _split3(x)
    return _dot(hi, a_b) + _dot(mid, a_b) + _dot(lo, a_b)


def _silu(x):
    return x * jax.nn.sigmoid(x)


def _ssd_body(ts, z_ref, xbc_ref, dt_ref, cw_ref, cb_ref, dtb_ref, alog_ref, dexp_ref, g_ref,
              tril_ref, ones_ref, expand_ref, eye_ref, trilrep_ref, bd_ref,
              o_ref, xe, state):
    ti = pl.program_id(1)

    @pl.when(ti == 0)
    def _():
        xe[0:8, :] = jnp.zeros((8, SSD_XBC), F32)
        state[...] = jnp.zeros(state.shape, F32)

    xe[8:8 + ts, :] = xbc_ref[0].astype(F32)
    conv = cb_ref[...] + cw_ref[0:1, :] * xe[5:5 + ts, :]
    for kk in range(1, SSD_CONV):
        conv = conv + cw_ref[kk:kk + 1, :] * xe[5 + kk:5 + kk + ts, :]
    tail = xe[ts:ts + 8, :]
    xe[0:8, :] = tail
    xc = _silu(conv)
    xs = xc[:, :D_MODEL]
    bm = xc[:, D_MODEL:D_MODEL + SSD_GROUPS * SSD_STATE].astype(BF16)
    cm = xc[:, D_MODEL + SSD_GROUPS * SSD_STATE:].astype(BF16)

    dt = jax.nn.softplus(dt_ref[0] + dtb_ref[...])
    a = -jnp.exp(alog_ref[...]) * dt
    acum = _dot3_rhs(tril_ref[...], a)
    colexp = _dot3_lhs(acum, expand_ref[...])
    dtexp = _dot3_lhs(dt, expand_ref[...])
    rowexp = _dot3_rhs(ones_ref[...], colexp * eye_ref[...])
    lmat = jnp.exp(jnp.where(trilrep_ref[...] > 0.5, colexp - rowexp, NEG_INF))
    xdt = xs * dtexp
    decay_in = jnp.exp(colexp)

    bdmask = bd_ref[...] > 0.5
    ys = []
    for c in range(ts // CHUNK):
        rows = slice(c * CHUNK, (c + 1) * CHUNK)
        a_last = colexp[(c + 1) * CHUNK - 1:(c + 1) * CHUNK, :]
        decay_st = jnp.exp(a_last - colexp[rows, :])
        chunk_decay = jnp.exp(a_last)
        xd = (xdt[rows, :] * decay_st).astype(BF16)
        xb = xdt[rows, :].astype(BF16)
        yg = []
        for g in range(SSD_GROUPS):
            lanes = slice(g * GROUP_LANES, (g + 1) * GROUP_LANES)
            cg = cm[rows, g * SSD_STATE:(g + 1) * SSD_STATE]
            bg = bm[rows, g * SSD_STATE:(g + 1) * SSD_STATE]
            cb = _dot_nt(cg, jnp.concatenate([bg] * 4, axis=0))
            w = (cb * lmat[rows, lanes]).astype(BF16)
            xg = xb[:, lanes]
            xbd = jnp.where(bdmask, jnp.concatenate([xg] * 4, axis=0), jnp.zeros((), BF16))
            y_diag = _dot(w, xbd)
            prev = state[g]
            y_off = _dot(cg, prev.astype(BF16)) * decay_in[rows, lanes]
            state[g] = prev * chunk_decay[:, lanes] + _dot_tn(bg, xd[:, lanes])
            yg.append(y_diag + y_off)
        ys.append(jnp.concatenate(yg, axis=1))
    y = jnp.concatenate(ys, axis=0) + dexp_ref[...] * xs
    y = y * _silu(z_ref[0].astype(F32))
    o_ref[0] = _rms(y, g_ref[...]).astype(BF16)


def _ssd_constants(ts):
    r = jnp.arange(ts)
    same_chunk = (r[:, None] // CHUNK) == (r[None, :] // CHUNK)
    tril = (same_chunk & (r[:, None] >= r[None, :])).astype(BF16)
    ones = same_chunk.astype(BF16)
    lane = jnp.arange(D_MODEL)
    expand = (jnp.arange(LANES)[:, None] == (lane[None, :] // SSD_HEAD_DIM)).astype(BF16)
    l_in = r[:, None] % CHUNK
    s_in = lane[None, :] % SSD_HEAD_DIM
    eye = (l_in == s_in).astype(F32)
    trilrep = (l_in >= s_in).astype(F32)
    q = jnp.arange(GROUP_LANES)
    bd = ((q[:, None] // SSD_HEAD_DIM) == (q[None, :] // SSD_HEAD_DIM)).astype(F32)
    return tril, ones, expand, eye, trilrep, bd


def _ssd(proj3, dt3, conv_w, conv_b, dt_bias, a_log, d_exp, g_ssd, ts=256):
    b, s, _ = proj3.shape
    consts = _ssd_constants(ts)
    full = lambda arr: pl.BlockSpec(arr.shape, lambda bi, ti: (0,) * arr.ndim)
    params = (conv_w, conv_b, dt_bias, a_log, d_exp, g_ssd)
    return pl.pallas_call(
        functools.partial(_ssd_body, ts),
        grid=(b, s // ts),
        in_specs=[
            pl.BlockSpec((1, ts, D_MODEL), lambda bi, ti: (bi, ti, COL_Z // D_MODEL)),
            pl.BlockSpec((1, ts, SSD_XBC), lambda bi, ti: (bi, ti, COL_XBC // SSD_XBC)),
            pl.BlockSpec((1, ts, LANES), lambda bi, ti: (bi, ti, 0)),
        ] + [full(p) for p in params] + [full(c) for c in consts],
        out_specs=pl.BlockSpec((1, ts, D_MODEL), lambda bi, ti: (bi, ti, 0)),
        out_shape=jax.ShapeDtypeStruct((b, s, D_MODEL), BF16),
        scratch_shapes=[
            pltpu.VMEM((ts + 8, SSD_XBC), F32),
            pltpu.VMEM((SSD_GROUPS, SSD_STATE, GROUP_LANES), F32),
        ],
        compiler_params=_cparams(("arbitrary", "arbitrary"), 48),
        name="ssd_mixer",
    )(proj3, proj3, dt3, *params, *consts)


def _merge_body(att_ref, ssd_ref, ga_ref, gb_ref, x_ref, wa_ref, wb_ref, wo_ref, o_ref):
    ya = _dot(att_ref[...], wa_ref[...])
    yb = _dot(ssd_ref[...], wb_ref[...])
    ga = jax.nn.sigmoid(ga_ref[...].astype(F32))
    gb = jax.nn.sigmoid(gb_ref[...].astype(F32))
    m = (ga * ya + gb * yb).astype(BF16)
    o_ref[...] = x_ref[...] + _dot(m, wo_ref[...])


def _merge(att2, ssd2, proj, x2, wa, wb, wo, tm=1024):
    t = x2.shape[0]
    row = lambda c: pl.BlockSpec((tm, D_MODEL), lambda i: (i, c))
    wspec = pl.BlockSpec((D_MODEL, D_MODEL), lambda i: (0, 0))
    gcol = COL_GATE // D_MODEL
    return pl.pallas_call(
        _merge_body,
        grid=(t // tm,),
        in_specs=[row(0), row(0), row(gcol), row(gcol + 1), row(0), wspec, wspec, wspec],
        out_specs=row(0),
        out_shape=jax.ShapeDtypeStruct((t, D_MODEL), F32),
        compiler_params=_cparams(("arbitrary",), 40),
        name="gated_merge",
    )(att2, ssd2, proj, proj, x2, wa, wb, wo)


SUBLANES = 8


def _batcher_network(n):
    def merge(lo, hi, r):
        step = r * 2
        if step < hi - lo:
            yield from merge(lo, hi, step)
            yield from merge(lo + r, hi, step)
            yield from [(i, i + r) for i in range(lo + r, hi - r, step)]
        else:
            yield (lo, lo + r)

    def sort(lo, hi):
        if hi - lo >= 1:
            mid = lo + (hi - lo) // 2
            yield from sort(lo, mid)
            yield from sort(mid + 1, hi)
            yield from merge(lo, hi, 1)

    return tuple(sort(0, n - 1))


_SORT16 = _batcher_network(PEER_TOPK)


def _exchange(xs, i, j):
    xs[i], xs[j] = jnp.maximum(xs[i], xs[j]), jnp.minimum(xs[i], xs[j])


def _all_sublanes(x, op):
    for shift in (4, 2, 1):
        x = op(x, pltpu.roll(x, shift, 0))
    return x


def _top16(s):
    xs = [s[SUBLANES * v:SUBLANES * (v + 1), :] for v in range(PEER_TOPK)]
    for i, j in _SORT16:
        _exchange(xs, i, j)
    for shift in (4, 2, 1):
        ys = [pltpu.roll(x, shift, 0) for x in xs]
        xs = [jnp.maximum(xs[k], ys[PEER_TOPK - 1 - k]) for k in range(PEER_TOPK)]
        for d in (8, 4, 2, 1):
            for i in range(PEER_TOPK):
                if i & d == 0:
                    _exchange(xs, i, i + d)
    return xs


def _stack(rows, sub):
    out = rows[SUBLANES - 1]
    for r in range(SUBLANES - 2, -1, -1):
        out = jnp.where(sub == r, rows[r], out)
    return out


def _candidate_groups(a_rep, a_hi, b_lo, b_hi, b0, combine):
    groups = [combine(a_rep[0], b_lo), combine(a_rep[0], b_hi)]
    groups += [combine(a_rep[a], b_lo) for a in range(1, SUBLANES)]
    groups.append(combine(a_hi, b0))
    return groups


def _route_body(x_ref, g_ref, wq_ref, keys_ref, ht_ref, a_ref, b_ref, tau_ref):
    hb = _rms(x_ref[...], g_ref[...])
    ht = hb.T.astype(BF16)
    ht_ref[...] = ht
    qt = _dot(wq_ref[...], ht).astype(BF16)
    tt = ht.shape[1]
    sub = lax.broadcasted_iota(jnp.int32, (SUBLANES, tt), 0)
    for n in range(PEER_HEADS):
        s1 = _dot(keys_ref[2 * n], qt[(2 * n) * PEER_HALF:(2 * n + 1) * PEER_HALF, :])
        s2 = _dot(keys_ref[2 * n + 1], qt[(2 * n + 1) * PEER_HALF:(2 * n + 2) * PEER_HALF, :])
        r1 = _top16(s1)
        r2 = _top16(s2)
        add = lambda p, q: p + q
        cand = _candidate_groups(r1, _stack(r1[SUBLANES:], sub), _stack(r2[:SUBLANES], sub),
                                 _stack(r2[SUBLANES:], sub), r2[0], add)
        work = cand
        for it in range(PEER_TOPK):
            m = functools.reduce(jnp.maximum, work)
            m = _all_sublanes(m, jnp.maximum)
            if it < PEER_TOPK - 1:
                work = [jnp.where(w == m, NEG_INF, w) for w in work]
        sel = [c >= m for c in cand]
        top = r1[0] + r2[0]
        z = functools.reduce(add, [jnp.where(s, jnp.exp(c - top), 0.0) for s, c in zip(sel, cand)])
        inv_z = GELU_OUT_SCALE / _all_sublanes(z, add)
        a_ref[n] = jnp.exp(s1 - r1[0][0:1]) * inv_z[0:1]
        b_ref[n] = jnp.exp(s2 - r2[0][0:1])
        pa = [jnp.exp(r - r1[0]) * inv_z for r in r1]
        pb = [jnp.exp(r - r2[0]) for r in r2]
        prod = _candidate_groups(pa, _stack(pa[SUBLANES:], sub), _stack(pb[:SUBLANES], sub),
                                 _stack(pb[SUBLANES:], sub), pb[0], lambda p, q: p * q)
        tau = functools.reduce(jnp.minimum, [jnp.where(s, p, jnp.inf) for s, p in zip(sel, prod)])
        tau_ref[n] = _all_sublanes(tau, jnp.minimum)[0:1]


def _route(x1, g_ffn, wq_t, keys, tt=512):
    t = x1.shape[0]
    return pl.pallas_call(
        _route_body,
        grid=(t // tt,),
        in_specs=[
            pl.BlockSpec((tt, D_MODEL), lambda i: (i, 0)),
            pl.BlockSpec((1, D_MODEL), lambda i: (0, 0)),
            pl.BlockSpec(wq_t.shape, lambda i: (0, 0)),
            pl.BlockSpec(keys.shape, lambda i: (0, 0, 0)),
        ],
        out_specs=[
            pl.BlockSpec((D_MODEL, tt), lambda i: (0, i)),
            pl.BlockSpec((PEER_HEADS, PEER_NKEYS, tt), lambda i: (0, 0, i)),
            pl.BlockSpec((PEER_HEADS, PEER_NKEYS, tt), lambda i: (0, 0, i)),
            pl.BlockSpec((PEER_HEADS, 1, tt), lambda i: (0, 0, i)),
        ],
        out_shape=[
            jax.ShapeDtypeStruct((D_MODEL, t), BF16),
            jax.ShapeDtypeStruct((PEER_HEADS, PEER_NKEYS, t), F32),
            jax.ShapeDtypeStruct((PEER_HEADS, PEER_NKEYS, t), F32),
            jax.ShapeDtypeStruct((PEER_HEADS, 1, t), F32),
        ],
        compiler_params=_cparams(("arbitrary",), 48),
        name="peer_route",
    )(x1, g_ffn, wq_t, keys)


def _experts_body(te, ht_ref, u_ref, vt_ref, a_ref, b_ref, tau_ref, x_ref, g_ref, o_ref, acc, gates, act):
    e = pl.program_id(1)
    nblk = te // PEER_NKEYS
    tt = ht_ref.shape[1]

    @pl.when(e == 0)
    def _():
        acc[...] = jnp.zeros(acc.shape, F32)

    a_rows = [a_ref[n] for n in range(PEER_HEADS)]
    for ib in range(nblk):
        rows = slice(ib * PEER_NKEYS, (ib + 1) * PEER_NKEYS)
        for lb in range(tt // LANES):
            lanes = slice(lb * LANES, (lb + 1) * LANES)
            gate = None
            for n in range(PEER_HEADS):
                p = a_rows[n][ib:ib + 1, lanes] * b_ref[n, :, lanes]
                term = jnp.where(p >= tau_ref[n, :, lanes], p, 0.0)
                gate = term if gate is None else gate + term
            gates[rows, lanes] = gate
    s = _dot(u_ref[...], ht_ref[...])
    act[...] = (gates[...] * (s * (1.0 + lax.erf(s)))).astype(BF16)
    acc[...] += _dot(vt_ref[...], act[...])

    @pl.when(e == pl.num_programs(1) - 1)
    def _():
        o_ref[...] = _rms(x_ref[...] + acc[...].T, g_ref[...])


def _experts(h2t, u_b, vt_b, a_f, b_f, tau, x1, g_final, tt=1024, te=1024):
    t = x1.shape[0]
    n_e = PEER_EXPERTS // te
    return pl.pallas_call(
        functools.partial(_experts_body, te),
        grid=(t // tt, n_e),
        in_specs=[
            pl.BlockSpec((D_MODEL, tt), lambda i, e: (0, i)),
            pl.BlockSpec((te, D_MODEL), lambda i, e: (e, 0)),
            pl.BlockSpec((D_MODEL, te), lambda i, e: (0, e)),
            pl.BlockSpec((PEER_HEADS, te // PEER_NKEYS, tt), lambda i, e: (0, e, i)),
            pl.BlockSpec((PEER_HEADS, PEER_NKEYS, tt), lambda i, e: (0, 0, i)),
            pl.BlockSpec((PEER_HEADS, 1, tt), lambda i, e: (0, 0, i)),
            pl.BlockSpec((tt, D_MODEL), lambda i, e: (i, 0)),
            pl.BlockSpec((1, D_MODEL), lambda i, e: (0, 0)),
        ],
        out_specs=pl.BlockSpec((tt, D_MODEL), lambda i, e: (i, 0)),
        out_shape=jax.ShapeDtypeStruct((t, D_MODEL), F32),
        scratch_shapes=[pltpu.VMEM((D_MODEL, tt), F32), pltpu.VMEM((te, tt), F32),
                        pltpu.VMEM((te, tt), BF16)],
        compiler_params=_cparams(("arbitrary", "arbitrary"), 56),
        name="peer_experts",
    )(h2t, u_b, vt_b, a_f, b_f, tau, x1, g_final)


def _lambda_init(layer_index):
    return 0.8 - 0.6 * math.exp(-0.3 * (layer_index - 1))


def _pad_lanes(v):
    return jnp.pad(v.reshape(1, -1), ((0, 0), (0, LANES - v.shape[-1])))


def _layer(x, layer, g_mix, w_in, lam_q1, lam_k1, lam_q2, lam_k2, g_subln, conv_w, conv_b, dt_bias,
           a_log, d_skip, g_ssd, w_branch_a, w_branch_b, w_out, g_ffn, w_query, sub_keys,
           expert_u, expert_v, g_out):
    b, s, d = x.shape
    t = b * s
    x2 = x.reshape(t, d)
    row = lambda v: v.reshape(1, -1)

    w = w_in[layer]
    dt_lo = COL_GATE
    q_scale = DA_HEAD_DIM ** -0.5 * LOG2E
    w_main = jnp.concatenate([w[:, :COL_K] * q_scale, w[:, COL_K:dt_lo], w[:, dt_lo + SSD_HEADS:]],
                             axis=1).astype(BF16)
    w_dt = jnp.pad(w[:, dt_lo:dt_lo + SSD_HEADS], ((0, 0), (0, LANES - SSD_HEADS))).astype(BF16)
    slopes = jnp.exp2(-8.0 * jnp.arange(1, DA_HEADS + 1, dtype=F32) / DA_HEADS)
    slopes = jnp.broadcast_to(slopes[:, None, None], (DA_HEADS, 1, LANES))

    proj, dt_raw = _inproj(x2, row(g_mix[layer]), w_main, w_dt)
    proj3 = proj.reshape(b, s, N_MAIN)

    att = _attention(proj3, slopes, row(lam_q1[layer]), row(lam_k1[layer]), row(lam_q2[layer]),
                     row(lam_k2[layer]), row(g_subln[layer]), _lambda_init(layer + 1))
    ssd = _ssd(proj3, dt_raw.reshape(b, s, LANES), conv_w[layer].reshape(SSD_CONV, SSD_XBC),
               row(conv_b[layer]), _pad_lanes(dt_bias[layer]), _pad_lanes(a_log[layer]),
               row(jnp.repeat(d_skip[layer], SSD_HEAD_DIM)), row(g_ssd[layer]))

    x1 = _merge(att.reshape(t, d), ssd.reshape(t, d), proj, x2, w_branch_a[layer].astype(BF16),
                w_branch_b[layer].astype(BF16), w_out[layer].astype(BF16))

    keys = sub_keys[layer].reshape(2 * PEER_HEADS, PEER_NKEYS, PEER_HALF).astype(BF16)
    h2t, a_f, b_f, tau = _route(x1, row(g_ffn[layer]), w_query[layer].T.astype(BF16), keys)
    out = _experts(h2t, (expert_u[layer] * GELU_ARG_SCALE).astype(BF16), expert_v[layer].T.astype(BF16), a_f, b_f, tau,
                   x1, row(g_out))
    return out.reshape(b, s, d)


def kernel(x, g_mix, w_in, lam_q1, lam_k1, lam_q2, lam_k2, g_subln, conv_w, conv_b, dt_bias, a_log,
           d_skip, g_ssd, w_branch_a, w_branch_b, w_out, g_ffn, w_query, sub_keys, expert_u, expert_v,
           g_final):
    depth = g_mix.shape[0]
    assert depth == 1, "the final RMSNorm is fused into the last (only) layer's expert kernel"
    return _layer(x, 0, g_mix, w_in, lam_q1, lam_k1, lam_q2, lam_k2, g_subln, conv_w, conv_b, dt_bias,
                  a_log, d_skip, g_ssd, w_branch_a, w_branch_b, w_out, g_ffn, w_query, sub_keys,
                  expert_u, expert_v, g_final)
```

```python
import functools
import math

import jax
import jax.numpy as jnp
from jax import lax
from jax.experimental import pallas as pl
from jax.experimental.pallas import tpu as pltpu

F32 = jnp.float32
BF16 = jnp.bfloat16

D_MODEL = 1024
CHUNK = 64
EPS = 1e-6
LOG2E = 1.4426950408889634

DA_HEADS = 8
DA_HEAD_DIM = 64
DA_V_DIM = 128

SSD_HEADS = 16
SSD_HEAD_DIM = 64
SSD_GROUPS = 4
SSD_STATE = 128
SSD_CONV = 4
SSD_XBC = 2048
GROUP_LANES = (SSD_HEADS // SSD_GROUPS) * SSD_HEAD_DIM

PEER_HEADS = 8
PEER_NKEYS = 128
PEER_EXPERTS = PEER_NKEYS * PEER_NKEYS
PEER_HALF = 128
PEER_TOPK = 16
GELU_ARG_SCALE = 1.0 / math.sqrt(2.0)
GELU_OUT_SCALE = 1.0 / math.sqrt(2.0)

N_MAIN = 8192
COL_Q, COL_K, COL_V, COL_Z, COL_XBC, COL_GATE = 0, 1024, 2048, 3072, 4096, 6144
LANES = 128

NEG_INF = float("-inf")


def _cparams(sem, vmem_mb):
    return pltpu.CompilerParams(dimension_semantics=sem, vmem_limit_bytes=vmem_mb * 1024 * 1024)


def _rms(x, g):
    return x * lax.rsqrt(jnp.mean(x * x, axis=-1, keepdims=True) + EPS) * g


def _dot(a, b):
    return jnp.dot(a, b, preferred_element_type=F32)


def _dot_nt(a, b):
    return lax.dot_general(a, b, (((1,), (1,)), ((), ())), preferred_element_type=F32)


def _dot_tn(a, b):
    return lax.dot_general(a, b, (((0,), (0,)), ((), ())), preferred_element_type=F32)


def _inproj_body(x_ref, g_ref, w_ref, wdt_ref, o_ref, dt_ref, h_scr):
    @pl.when(pl.program_id(1) == 0)
    def _():
        hb = _rms(x_ref[...], g_ref[...]).astype(BF16)
        h_scr[...] = hb
        dt_ref[...] = _dot(hb, wdt_ref[...])

    o_ref[...] = _dot(h_scr[...], w_ref[...]).astype(BF16)


def _inproj(x2, g_mix, w_main, w_dt, tm=1024, tn=1024):
    t = x2.shape[0]
    return pl.pallas_call(
        _inproj_body,
        grid=(t // tm, N_MAIN // tn),
        in_specs=[
            pl.BlockSpec((tm, D_MODEL), lambda i, j: (i, 0)),
            pl.BlockSpec((1, D_MODEL), lambda i, j: (0, 0)),
            pl.BlockSpec((D_MODEL, tn), lambda i, j: (0, j)),
            pl.BlockSpec((D_MODEL, LANES), lambda i, j: (0, 0)),
        ],
        out_specs=[
            pl.BlockSpec((tm, tn), lambda i, j: (i, j)),
            pl.BlockSpec((tm, LANES), lambda i, j: (i, 0)),
        ],
        out_shape=[
            jax.ShapeDtypeStruct((t, N_MAIN), BF16),
            jax.ShapeDtypeStruct((t, LANES), F32),
        ],
        scratch_shapes=[pltpu.VMEM((tm, D_MODEL), BF16)],
        compiler_params=_cparams(("arbitrary", "arbitrary"), 40),
        name="inproj",
    )(x2, g_mix, w_main, w_dt)


ATT_ROWS = DA_V_DIM + 16


def _attn_body(lam0, tq, slope_ref, lq1_ref, lk1_ref, lq2_ref, lk2_ref, gsub_ref,
               q_ref, k_ref, v_ref, o_ref,
               bdiag, ka, kb_, vt, acc1, acc2, ua, ub):
    qi = pl.program_id(2)
    slope2 = slope_ref[0][:, 0:1] * LOG2E
    n_kv = v_ref.shape[1] // tq

    @pl.when(qi == 0)
    def _():
        jj = lax.broadcasted_iota(jnp.int32, (tq, tq), 0)
        ii = lax.broadcasted_iota(jnp.int32, (tq, tq), 1)
        d = (ii - jj).astype(F32)
        visible = (jj // CHUNK) <= (ii // CHUNK)
        bdiag[...] = jnp.where(visible, -slope2 * (jnp.abs(d) - d), NEG_INF)

        lane = lax.broadcasted_iota(jnp.int32, (tq, LANES), 1)
        c = slope2 * lax.broadcasted_iota(jnp.int32, (tq, LANES), 0).astype(F32)
        c_hi = c.astype(BF16).astype(F32)
        c_mid = (c - c_hi).astype(BF16).astype(F32)
        c_lo = c - c_hi - c_mid

        def extras(base):
            return jnp.where(lane == base, c_hi, jnp.where(lane == base + 1, c_mid,
                             jnp.where(lane == base + 2, c_lo, 0.0)))

        ext_a, ext_b = extras(DA_HEAD_DIM), extras(0)
        ones_row = lax.broadcasted_iota(jnp.int32, (ATT_ROWS - DA_V_DIM, tq), 0) == 0
        for cblk in range(n_kv):
            rows = slice(cblk * tq, (cblk + 1) * tq)
            kblk = k_ref[0, rows, :].astype(F32)
            ka[rows, :] = jnp.where(lane < DA_HEAD_DIM, kblk, ext_a).astype(BF16)
            kb_[rows, :] = jnp.where(lane >= DA_HEAD_DIM, kblk, ext_b).astype(BF16)
            vt[0:DA_V_DIM, rows] = v_ref[0, rows, :].astype(F32).T.astype(BF16)
            vt[DA_V_DIM:, rows] = jnp.where(ones_row, 1.0, 0.0).astype(BF16)

    qt = q_ref[0].astype(F32).T
    row = lax.broadcasted_iota(jnp.int32, qt.shape, 0)
    qa = jnp.where(row < DA_HEAD_DIM, qt, jnp.where(row < DA_HEAD_DIM + 3, 1.0, 0.0)).astype(BF16)
    qb = jnp.where(row >= DA_HEAD_DIM, qt, jnp.where(row < 3, 1.0, 0.0)).astype(BF16)

    streams = ((ka, qa, acc1), (kb_, qb, acc2))

    def scores(start, rows, dst, lanes=slice(None)):
        for mi, (kref, qm, _) in enumerate(streams):
            dst[mi, :, lanes] = _dot(kref[pl.ds(start, rows), :], qm[:, lanes])

    def absorb(src, start, bias, shift, ms, lanes=slice(None)):
        v_t = vt[:, pl.ds(start, hk)]
        out = []
        for mi, ((_, _, acc), m_old) in enumerate(zip(streams, ms)):
            u = src[mi, :, lanes]
            m_old = m_old[:, lanes]
            if bias is not None:
                u = u + bias
            m_new = jnp.maximum(m_old, jnp.max(u, axis=0, keepdims=True) + shift)
            p = jnp.exp2(u - (m_new - shift))
            pv = _dot(v_t, p.astype(BF16))
            acc[:, lanes] = jnp.exp2(m_old - m_new) * acc[:, lanes] + pv
            out.append(m_new)
        return tuple(out)

    hk = tq // 2
    acc1[...] = jnp.zeros(acc1.shape, F32)
    acc2[...] = jnp.zeros(acc2.shape, F32)
    neg = jnp.full((1, tq), NEG_INF, F32)
    scores(0, hk, ua)

    def body(j, ms):
        s0 = pl.multiple_of(2 * j * hk, hk)
        shift = -slope2 * ((qi - j) * tq).astype(F32)
        scores(s0 + hk, hk, ub)
        ms = absorb(ua, s0, None, shift, ms)
        scores(s0 + 2 * hk, hk, ua)
        return absorb(ub, s0 + hk, None, shift, ms)

    ms = lax.fori_loop(0, qi, body, (neg, neg))
    diag = pl.multiple_of(qi * tq, tq)
    no_shift = jnp.zeros((1, 1), F32)
    late = slice(hk, tq)
    scores(diag + hk, hk, ub, late)
    ms = absorb(ua, diag, bdiag[0:hk, :], no_shift, ms)
    absorb(ub, diag + hk, bdiag[hk:tq, late], no_shift, ms, late)

    lam = (jnp.exp(jnp.sum(lq1_ref[...] * lk1_ref[...], axis=-1, keepdims=True))
           - jnp.exp(jnp.sum(lq2_ref[...] * lk2_ref[...], axis=-1, keepdims=True)) + lam0)
    o = (acc1[0:DA_V_DIM, :] / acc1[DA_V_DIM:DA_V_DIM + 1, :]
         - lam * (acc2[0:DA_V_DIM, :] / acc2[DA_V_DIM:DA_V_DIM + 1, :]))
    o = o * lax.rsqrt(jnp.mean(o * o, axis=0, keepdims=True) + EPS)
    o_ref[0] = (o.T * (gsub_ref[...] * (1.0 - lam0))).astype(BF16)


def _attention(proj3, slopes, lq1, lk1, lq2, lk2, g_subln, lam0, tq=1024):
    b, s, _ = proj3.shape
    kcol, vcol = COL_K // LANES, COL_V // LANES
    small = lambda shape: pl.BlockSpec(shape, lambda bi, h, qi: (0,) * len(shape))
    return pl.pallas_call(
        functools.partial(_attn_body, lam0, tq),
        grid=(b, DA_HEADS, s // tq),
        in_specs=[
            pl.BlockSpec((1, 1, LANES), lambda bi, h, qi: (h, 0, 0)),
            small((1, DA_HEAD_DIM)), small((1, DA_HEAD_DIM)),
            small((1, DA_HEAD_DIM)), small((1, DA_HEAD_DIM)),
            small((1, DA_V_DIM)),
            pl.BlockSpec((1, tq, LANES), lambda bi, h, qi: (bi, qi, h)),
            pl.BlockSpec((1, s, LANES), lambda bi, h, qi: (bi, 0, kcol + h)),
            pl.BlockSpec((1, s, LANES), lambda bi, h, qi: (bi, 0, vcol + h)),
        ],
        out_specs=pl.BlockSpec((1, tq, LANES), lambda bi, h, qi: (bi, qi, h)),
        out_shape=jax.ShapeDtypeStruct((b, s, DA_HEADS * DA_V_DIM), BF16),
        scratch_shapes=[
            pltpu.VMEM((tq, tq), F32),
            pltpu.VMEM((s, LANES), BF16), pltpu.VMEM((s, LANES), BF16),
            pltpu.VMEM((ATT_ROWS, s), BF16),
            pltpu.VMEM((ATT_ROWS, tq), F32), pltpu.VMEM((ATT_ROWS, tq), F32),
            pltpu.VMEM((2, tq // 2, tq), F32), pltpu.VMEM((2, tq // 2, tq), F32),
        ],
        compiler_params=_cparams(("arbitrary", "arbitrary", "arbitrary"), 40),
        name="diff_attention",
    )(slopes, lq1, lk1, lq2, lk2, g_subln, proj3, proj3, proj3)


def _split3(x):
    hi = x.astype(BF16)
    r = x - hi.astype(F32)
    mid = r.astype(BF16)
    lo = (r - mid.astype(F32)).astype(BF16)
    return hi, mid, lo


def _dot3_rhs(a_b, x):
    hi, mid, lo = _split3(x)
    return _dot(a_b, hi) + _dot(a_b, mid) + _dot(a_b, lo)


def _dot3_lhs(x, a_b):
    hi, mid, lo = _split3(x)
    return _dot(hi, a_b) + _dot(mid, a_b) + _dot(lo, a_b)


def _silu(x):
    return x * jax.nn.sigmoid(x)


def _ssd_body(ts, z_ref, xbc_ref, dt_ref, cw_ref, cb_ref, dtb_ref, alog_ref, dexp_ref, g_ref,
              tril_ref, ones_ref, expand_ref, eye_ref, trilrep_ref, bd_ref,
              o_ref, xe, state):
    ti = pl.program_id(1)

    @pl.when(ti == 0)
    def _():
        xe[0:8, :] = jnp.zeros((8, SSD_XBC), F32)
        state[...] = jnp.zeros(state.shape, F32)

    xe[8:8 + ts, :] = xbc_ref[0].astype(F32)
    conv = cb_ref[...] + cw_ref[0:1, :] * xe[5:5 + ts, :]
    for kk in range(1, SSD_CONV):
        conv = conv + cw_ref[kk:kk + 1, :] * xe[5 + kk:5 + kk + ts, :]
    tail = xe[ts:ts + 8, :]
    xe[0:8, :] = tail
    xc = _silu(conv)
    xs = xc[:, :D_MODEL]
    bm = xc[:, D_MODEL:D_MODEL + SSD_GROUPS * SSD_STATE].astype(BF16)
    cm = xc[:, D_MODEL + SSD_GROUPS * SSD_STATE:].astype(BF16)

    dt = jax.nn.softplus(dt_ref[0] + dtb_ref[...])
    a = -jnp.exp(alog_ref[...]) * dt
    acum = _dot3_rhs(tril_ref[...], a)
    colexp = _dot3_lhs(acum, expand_ref[...])
    dtexp = _dot3_lhs(dt, expand_ref[...])
    rowexp = _dot3_rhs(ones_ref[...], colexp * eye_ref[...])
    lmat = jnp.exp(jnp.where(trilrep_ref[...] > 0.5, colexp - rowexp, NEG_INF))
    xdt = xs * dtexp
    decay_in = jnp.exp(colexp)

    bdmask = bd_ref[...] > 0.5
    ys = []
    for c in range(ts // CHUNK):
        rows = slice(c * CHUNK, (c + 1) * CHUNK)
        a_last = colexp[(c + 1) * CHUNK - 1:(c + 1) * CHUNK, :]
        decay_st = jnp.exp(a_last - colexp[rows, :])
        chunk_decay = jnp.exp(a_last)
        xd = (xdt[rows, :] * decay_st).astype(BF16)
        xb = xdt[rows, :].astype(BF16)
        yg = []
        for g in range(SSD_GROUPS):
            lanes = slice(g * GROUP_LANES, (g + 1) * GROUP_LANES)
            cg = cm[rows, g * SSD_STATE:(g + 1) * SSD_STATE]
            bg = bm[rows, g * SSD_STATE:(g + 1) * SSD_STATE]
            cb = _dot_nt(cg, jnp.concatenate([bg] * 4, axis=0))
            w = (cb * lmat[rows, lanes]).astype(BF16)
            xg = xb[:, lanes]
            xbd = jnp.where(bdmask, jnp.concatenate([xg] * 4, axis=0), jnp.zeros((), BF16))
            y_diag = _dot(w, xbd)
            prev = state[g]
            y_off = _dot(cg, prev.astype(BF16)) * decay_in[rows, lanes]
            state[g] = prev * chunk_decay[:, lanes] + _dot_tn(bg, xd[:, lanes])
            yg.append(y_diag + y_off)
        ys.append(jnp.concatenate(yg, axis=1))
    y = jnp.concatenate(ys, axis=0) + dexp_ref[...] * xs
    y = y * _silu(z_ref[0].astype(F32))
    o_ref[0] = _rms(y, g_ref[...]).astype(BF16)


def _ssd_constants(ts):
    r = jnp.arange(ts)
    same_chunk = (r[:, None] // CHUNK) == (r[None, :] // CHUNK)
    tril = (same_chunk & (r[:, None] >= r[None, :])).astype(BF16)
    ones = same_chunk.astype(BF16)
    lane = jnp.arange(D_MODEL)
    expand = (jnp.arange(LANES)[:, None] == (lane[None, :] // SSD_HEAD_DIM)).astype(BF16)
    l_in = r[:, None] % CHUNK
    s_in = lane[None, :] % SSD_HEAD_DIM
    eye = (l_in == s_in).astype(F32)
    trilrep = (l_in >= s_in).astype(F32)
    q = jnp.arange(GROUP_LANES)
    bd = ((q[:, None] // SSD_HEAD_DIM) == (q[None, :] // SSD_HEAD_DIM)).astype(F32)
    return tril, ones, expand, eye, trilrep, bd


def _ssd(proj3, dt3, conv_w, conv_b, dt_bias, a_log, d_exp, g_ssd, ts=256):
    b, s, _ = proj3.shape
    consts = _ssd_constants(ts)
    full = lambda arr: pl.BlockSpec(arr.shape, lambda bi, ti: (0,) * arr.ndim)
    params = (conv_w, conv_b, dt_bias, a_log, d_exp, g_ssd)
    return pl.pallas_call(
        functools.partial(_ssd_body, ts),
        grid=(b, s // ts),
        in_specs=[
            pl.BlockSpec((1, ts, D_MODEL), lambda bi, ti: (bi, ti, COL_Z // D_MODEL)),
            pl.BlockSpec((1, ts, SSD_XBC), lambda bi, ti: (bi, ti, COL_XBC // SSD_XBC)),
            pl.BlockSpec((1, ts, LANES), lambda bi, ti: (bi, ti, 0)),
        ] + [full(p) for p in params] + [full(c) for c in consts],
        out_specs=pl.BlockSpec((1, ts, D_MODEL), lambda bi, ti: (bi, ti, 0)),
        out_shape=jax.ShapeDtypeStruct((b, s, D_MODEL), BF16),
        scratch_shapes=[
            pltpu.VMEM((ts + 8, SSD_XBC), F32),
            pltpu.VMEM((SSD_GROUPS, SSD_STATE, GROUP_LANES), F32),
        ],
        compiler_params=_cparams(("arbitrary", "arbitrary"), 48),
        name="ssd_mixer",
    )(proj3, proj3, dt3, *params, *consts)


def _merge_body(att_ref, ssd_ref, ga_ref, gb_ref, x_ref, wa_ref, wb_ref, wo_ref, o_ref):
    ya = _dot(att_ref[...], wa_ref[...])
    yb = _dot(ssd_ref[...], wb_ref[...])
    ga = jax.nn.sigmoid(ga_ref[...].astype(F32))
    gb = jax.nn.sigmoid(gb_ref[...].astype(F32))
    m = (ga * ya + gb * yb).astype(BF16)
    o_ref[...] = x_ref[...] + _dot(m, wo_ref[...])


def _merge(att2, ssd2, proj, x2, wa, wb, wo, tm=1024):
    t = x2.shape[0]
    row = lambda c: pl.BlockSpec((tm, D_MODEL), lambda i: (i, c))
    wspec = pl.BlockSpec((D_MODEL, D_MODEL), lambda i: (0, 0))
    gcol = COL_GATE // D_MODEL
    return pl.pallas_call(
        _merge_body,
        grid=(t // tm,),
        in_specs=[row(0), row(0), row(gcol), row(gcol + 1), row(0), wspec, wspec, wspec],
        out_specs=row(0),
        out_shape=jax.ShapeDtypeStruct((t, D_MODEL), F32),
        compiler_params=_cparams(("arbitrary",), 40),
        name="gated_merge",
    )(att2, ssd2, proj, proj, x2, wa, wb, wo)


SUBLANES = 8


def _batcher_network(n):
    def merge(lo, hi, r):
        step = r * 2
        if step < hi - lo:
            yield from merge(lo, hi, step)
            yield from merge(lo + r, hi, step)
            yield from [(i, i + r) for i in range(lo + r, hi - r, step)]
        else:
            yield (lo, lo + r)

    def sort(lo, hi):
        if hi - lo >= 1:
            mid = lo + (hi - lo) // 2
            yield from sort(lo, mid)
            yield from sort(mid + 1, hi)
            yield from merge(lo, hi, 1)

    return tuple(sort(0, n - 1))


_SORT16 = _batcher_network(PEER_TOPK)


def _exchange(xs, i, j):
    xs[i], xs[j] = jnp.maximum(xs[i], xs[j]), jnp.minimum(xs[i], xs[j])


def _all_sublanes(x, op):
    for shift in (4, 2, 1):
        x = op(x, pltpu.roll(x, shift, 0))
    return x


def _top16(s):
    xs = [s[SUBLANES * v:SUBLANES * (v + 1), :] for v in range(PEER_TOPK)]
    for i, j in _SORT16:
        _exchange(xs, i, j)
    for shift in (4, 2, 1):
        ys = [pltpu.roll(x, shift, 0) for x in xs]
        xs = [jnp.maximum(xs[k], ys[PEER_TOPK - 1 - k]) for k in range(PEER_TOPK)]
        for d in (8, 4, 2, 1):
            for i in range(PEER_TOPK):
                if i & d == 0:
                    _exchange(xs, i, i + d)
    return xs


def _stack(rows, sub):
    out = rows[SUBLANES - 1]
    for r in range(SUBLANES - 2, -1, -1):
        out = jnp.where(sub == r, rows[r], out)
    return out


def _candidate_groups(a_rep, a_hi, b_lo, b_hi, b0, combine):
    groups = [combine(a_rep[0], b_lo), combine(a_rep[0], b_hi)]
    groups += [combine(a_rep[a], b_lo) for a in range(1, SUBLANES)]
    groups.append(combine(a_hi, b0))
    return groups


def _route_body(x_ref, g_ref, wq_ref, keys_ref, ht_ref, a_ref, b_ref, tau_ref):
    hb = _rms(x_ref[...], g_ref[...])
    ht = hb.T.astype(BF16)
    ht_ref[...] = ht
    qt = _dot(wq_ref[...], ht).astype(BF16)
    tt = ht.shape[1]
    sub = lax.broadcasted_iota(jnp.int32, (SUBLANES, tt), 0)
    for n in range(PEER_HEADS):
        s1 = _dot(keys_ref[2 * n], qt[(2 * n) * PEER_HALF:(2 * n + 1) * PEER_HALF, :])
        s2 = _dot(keys_ref[2 * n + 1], qt[(2 * n + 1) * PEER_HALF:(2 * n + 2) * PEER_HALF, :])
        r1 = _top16(s1)
        r2 = _top16(s2)
        add = lambda p, q: p + q
        cand = _candidate_groups(r1, _stack(r1[SUBLANES:], sub), _stack(r2[:SUBLANES], sub),
                                 _stack(r2[SUBLANES:], sub), r2[0], add)
        work = cand
        for it in range(PEER_TOPK):
            m = functools.reduce(jnp.maximum, work)
            m = _all_sublanes(m, jnp.maximum)
            if it < PEER_TOPK - 1:
                work = [jnp.where(w == m, NEG_INF, w) for w in work]
        sel = [c >= m for c in cand]
        top = r1[0] + r2[0]
        z = functools.reduce(add, [jnp.where(s, jnp.exp(c - top), 0.0) for s, c in zip(sel, cand)])
        inv_z = GELU_OUT_SCALE / _all_sublanes(z, add)
        a_ref[n] = jnp.exp(s1 - r1[0][0:1]) * inv_z[0:1]
        b_ref[n] = jnp.exp(s2 - r2[0][0:1])
        pa = [jnp.exp(r - r1[0]) * inv_z for r in r1]
        pb = [jnp.exp(r - r2[0]) for r in r2]
        prod = _candidate_groups(pa, _stack(pa[SUBLANES:], sub), _stack(pb[:SUBLANES], sub),
                                 _stack(pb[SUBLANES:], sub), pb[0], lambda p, q: p * q)
        tau = functools.reduce(jnp.minimum, [jnp.where(s, p, jnp.inf) for s, p in zip(sel, prod)])
        tau_ref[n] = _all_sublanes(tau, jnp.minimum)[0:1]


def _route(x1, g_ffn, wq_t, keys, tt=512):
    t = x1.shape[0]
    return pl.pallas_call(
        _route_body,
        grid=(t // tt,),
        in_specs=[
            pl.BlockSpec((tt, D_MODEL), lambda i: (i, 0)),
            pl.BlockSpec((1, D_MODEL), lambda i: (0, 0)),
            pl.BlockSpec(wq_t.shape, lambda i: (0, 0)),
            pl.BlockSpec(keys.shape, lambda i: (0, 0, 0)),
        ],
        out_specs=[
            pl.BlockSpec((D_MODEL, tt), lambda i: (0, i)),
            pl.BlockSpec((PEER_HEADS, PEER_NKEYS, tt), lambda i: (0, 0, i)),
            pl.BlockSpec((PEER_HEADS, PEER_NKEYS, tt), lambda i: (0, 0, i)),
            pl.BlockSpec((PEER_HEADS, 1, tt), lambda i: (0, 0, i)),
        ],
        out_shape=[
            jax.ShapeDtypeStruct((D_MODEL, t), BF16),
            jax.ShapeDtypeStruct((PEER_HEADS, PEER_NKEYS, t), F32),
            jax.ShapeDtypeStruct((PEER_HEADS, PEER_NKEYS, t), F32),
            jax.ShapeDtypeStruct((PEER_HEADS, 1, t), F32),
        ],
        compiler_params=_cparams(("arbitrary",), 48),
        name="peer_route",
    )(x1, g_ffn, wq_t, keys)


def _experts_body(te, ht_ref, u_ref, vt_ref, a_ref, b_ref, tau_ref, x_ref, g_ref, o_ref, acc, gates, act):
    e = pl.program_id(1)
    nblk = te // PEER_NKEYS
    tt = ht_ref.shape[1]

    @pl.when(e == 0)
    def _():
        acc[...] = jnp.zeros(acc.shape, F32)

    a_rows = [a_ref[n] for n in range(PEER_HEADS)]
    for ib in range(nblk):
        rows = slice(ib * PEER_NKEYS, (ib + 1) * PEER_NKEYS)
        for lb in range(tt // LANES):
            lanes = slice(lb * LANES, (lb + 1) * LANES)
            gate = None
            for n in range(PEER_HEADS):
                p = a_rows[n][ib:ib + 1, lanes] * b_ref[n, :, lanes]
                term = jnp.where(p >= tau_ref[n, :, lanes], p, 0.0)
                gate = term if gate is None else gate + term
            gates[rows, lanes] = gate
    s = _dot(u_ref[...], ht_ref[...])
    act[...] = (gates[...] * (s * (1.0 + lax.erf(s)))).astype(BF16)
    acc[...] += _dot(vt_ref[...], act[...])

    @pl.when(e == pl.num_programs(1) - 1)
    def _():
        o_ref[...] = _rms(x_ref[...] + acc[...].T, g_ref[...])


def _experts(h2t, u_b, vt_b, a_f, b_f, tau, x1, g_final, tt=1024, te=1024):
    t = x1.shape[0]
    n_e = PEER_EXPERTS // te
    return pl.pallas_call(
        functools.partial(_experts_body, te),
        grid=(t // tt, n_e),
        in_specs=[
            pl.BlockSpec((D_MODEL, tt), lambda i, e: (0, i)),
            pl.BlockSpec((te, D_MODEL), lambda i, e: (e, 0)),
            pl.BlockSpec((D_MODEL, te), lambda i, e: (0, e)),
            pl.BlockSpec((PEER_HEADS, te // PEER_NKEYS, tt), lambda i, e: (0, e, i)),
            pl.BlockSpec((PEER_HEADS, PEER_NKEYS, tt), lambda i, e: (0, 0, i)),
            pl.BlockSpec((PEER_HEADS, 1, tt), lambda i, e: (0, 0, i)),
            pl.BlockSpec((tt, D_MODEL), lambda i, e: (i, 0)),
            pl.BlockSpec((1, D_MODEL), lambda i, e: (0, 0)),
        ],
        out_specs=pl.BlockSpec((tt, D_MODEL), lambda i, e: (i, 0)),
        out_shape=jax.ShapeDtypeStruct((t, D_MODEL), F32),
        scratch_shapes=[pltpu.VMEM((D_MODEL, tt), F32), pltpu.VMEM((te, tt), F32),
                        pltpu.VMEM((te, tt), BF16)],
        compiler_params=_cparams(("arbitrary", "arbitrary"), 56),
        name="peer_experts",
    )(h2t, u_b, vt_b, a_f, b_f, tau, x1, g_final)


def _lambda_init(layer_index):
    return 0.8 - 0.6 * math.exp(-0.3 * (layer_index - 1))


def _pad_lanes(v):
    return jnp.pad(v.reshape(1, -1), ((0, 0), (0, LANES - v.shape[-1])))


def _layer(x, layer, g_mix, w_in, lam_q1, lam_k1, lam_q2, lam_k2, g_subln, conv_w, conv_b, dt_bias,
           a_log, d_skip, g_ssd, w_branch_a, w_branch_b, w_out, g_ffn, w_query, sub_keys,
           expert_u, expert_v, g_out):
    b, s, d = x.shape
    t = b * s
    x2 = x.reshape(t, d)
    row = lambda v: v.reshape(1, -1)

    w = w_in[layer]
    dt_lo = COL_GATE
    q_scale = DA_HEAD_DIM ** -0.5 * LOG2E
    w_main = jnp.concatenate([w[:, :COL_K] * q_scale, w[:, COL_K:dt_lo], w[:, dt_lo + SSD_HEADS:]],
                             axis=1).astype(BF16)
    w_dt = jnp.pad(w[:, dt_lo:dt_lo + SSD_HEADS], ((0, 0), (0, LANES - SSD_HEADS))).astype(BF16)
    slopes = jnp.exp2(-8.0 * jnp.arange(1, DA_HEADS + 1, dtype=F32) / DA_HEADS)
    slopes = jnp.broadcast_to(slopes[:, None, None], (DA_HEADS, 1, LANES))

    proj, dt_raw = _inproj(x2, row(g_mix[layer]), w_main, w_dt)
    proj3 = proj.reshape(b, s, N_MAIN)

    att = _attention(proj3, slopes, row(lam_q1[layer]), row(lam_k1[layer]), row(lam_q2[layer]),
                     row(lam_k2[layer]), row(g_subln[layer]), _lambda_init(layer + 1))
    ssd = _ssd(proj3, dt_raw.reshape(b, s, LANES), conv_w[layer].reshape(SSD_CONV, SSD_XBC),
               row(conv_b[layer]), _pad_lanes(dt_bias[layer]), _pad_lanes(a_log[layer]),
               row(jnp.repeat(d_skip[layer], SSD_HEAD_DIM)), row(g_ssd[layer]))

    x1 = _merge(att.reshape(t, d), ssd.reshape(t, d), proj, x2, w_branch_a[layer].astype(BF16),
                w_branch_b[layer].astype(BF16), w_out[layer].astype(BF16))

    keys = sub_keys[layer].reshape(2 * PEER_HEADS, PEER_NKEYS, PEER_HALF).astype(BF16)
    h2t, a_f, b_f, tau = _route(x1, row(g_ffn[layer]), w_query[layer].T.astype(BF16), keys)
    out = _experts(h2t, (expert_u[layer] * GELU_ARG_SCALE).astype(BF16), expert_v[layer].T.astype(BF16), a_f, b_f, tau,
                   x1, row(g_out))
    return out.reshape(b, s, d)


def kernel(x, g_mix, w_in, lam_q1, lam_k1, lam_q2, lam_k2, g_subln, conv_w, conv_b, dt_bias, a_log,
           d_skip, g_ssd, w_branch_a, w_branch_b, w_out, g_ffn, w_query, sub_keys, expert_u, expert_v,
           g_final):
    depth = g_mix.shape[0]
    assert depth == 1, "the final RMSNorm is fused into the last (only) layer's expert kernel"
    return _layer(x, 0, g_mix, w_in, lam_q1, lam_k1, lam_q2, lam_k2, g_subln, conv_w, conv_b, dt_bias,
                  a_log, d_skip, g_ssd, w_branch_a, w_branch_b, w_out, g_ffn, w_query, sub_keys,
                  expert_u, expert_v, g_final)
```

```python
import functools
import math

import jax
import jax.numpy as jnp
from jax import lax
from jax.experimental import pallas as pl
from jax.experimental.pallas import tpu as pltpu

F32 = jnp.float32
BF16 = jnp.bfloat16

D_MODEL = 1024
CHUNK = 64
EPS = 1e-6
LOG2E = 1.4426950408889634

DA_HEADS = 8
DA_HEAD_DIM = 64
DA_V_DIM = 128

SSD_HEADS = 16
SSD_HEAD_DIM = 64
SSD_GROUPS = 4
SSD_STATE = 128
SSD_CONV = 4
SSD_XBC = 2048
GROUP_LANES = (SSD_HEADS // SSD_GROUPS) * SSD_HEAD_DIM

PEER_HEADS = 8
PEER_NKEYS = 128
PEER_EXPERTS = PEER_NKEYS * PEER_NKEYS
PEER_HALF = 128
PEER_TOPK = 16
GELU_ARG_SCALE = 1.0 / math.sqrt(2.0)
GELU_OUT_SCALE = 1.0 / math.sqrt(2.0)

N_MAIN = 8192
COL_Q, COL_K, COL_V, COL_Z, COL_XBC, COL_GATE = 0, 1024, 2048, 3072, 4096, 6144
LANES = 128

NEG_INF = float("-inf")


def _cparams(sem, vmem_mb):
    return pltpu.CompilerParams(dimension_semantics=sem, vmem_limit_bytes=vmem_mb * 1024 * 1024)


def _rms(x, g):
    return x * lax.rsqrt(jnp.mean(x * x, axis=-1, keepdims=True) + EPS) * g


def _dot(a, b):
    return jnp.dot(a, b, preferred_element_type=F32)


def _dot_nt(a, b):
    return lax.dot_general(a, b, (((1,), (1,)), ((), ())), preferred_element_type=F32)


def _dot_tn(a, b):
    return lax.dot_general(a, b, (((0,), (0,)), ((), ())), preferred_element_type=F32)


def _inproj_body(x_ref, g_ref, w_ref, wdt_ref, o_ref, dt_ref, h_scr):
    @pl.when(pl.program_id(1) == 0)
    def _():
        hb = _rms(x_ref[...], g_ref[...]).astype(BF16)
        h_scr[...] = hb
        dt_ref[...] = _dot(hb, wdt_ref[...])

    o_ref[...] = _dot(h_scr[...], w_ref[...]).astype(BF16)


def _inproj(x2, g_mix, w_main, w_dt, tm=1024, tn=1024):
    t = x2.shape[0]
    return pl.pallas_call(
        _inproj_body,
        grid=(t // tm, N_MAIN // tn),
        in_specs=[
            pl.BlockSpec((tm, D_MODEL), lambda i, j: (i, 0)),
            pl.BlockSpec((1, D_MODEL), lambda i, j: (0, 0)),
            pl.BlockSpec((D_MODEL, tn), lambda i, j: (0, j)),
            pl.BlockSpec((D_MODEL, LANES), lambda i, j: (0, 0)),
        ],
        out_specs=[
            pl.BlockSpec((tm, tn), lambda i, j: (i, j)),
            pl.BlockSpec((tm, LANES), lambda i, j: (i, 0)),
        ],
        out_shape=[
            jax.ShapeDtypeStruct((t, N_MAIN), BF16),
            jax.ShapeDtypeStruct((t, LANES), F32),
        ],
        scratch_shapes=[pltpu.VMEM((tm, D_MODEL), BF16)],
        compiler_params=_cparams(("arbitrary", "arbitrary"), 40),
        name="inproj",
    )(x2, g_mix, w_main, w_dt)


ATT_ROWS = DA_V_DIM + 16


def _attn_body(lam0, tq, slope_ref, lq1_ref, lk1_ref, lq2_ref, lk2_ref, gsub_ref,
               q_ref, k_ref, v_ref, o_ref,
               bdiag, ka, kb_, vt, acc1, acc2, ua, ub):
    qi = pl.program_id(2)
    slope2 = slope_ref[0][:, 0:1] * LOG2E
    n_kv = v_ref.shape[1] // tq

    @pl.when(qi == 0)
    def _():
        jj = lax.broadcasted_iota(jnp.int32, (tq, tq), 0)
        ii = lax.broadcasted_iota(jnp.int32, (tq, tq), 1)
        d = (ii - jj).astype(F32)
        visible = (jj // CHUNK) <= (ii // CHUNK)
        bdiag[...] = jnp.where(visible, -slope2 * (jnp.abs(d) - d), NEG_INF)

        lane = lax.broadcasted_iota(jnp.int32, (tq, LANES), 1)
        c = slope2 * lax.broadcasted_iota(jnp.int32, (tq, LANES), 0).astype(F32)
        c_hi = c.astype(BF16).astype(F32)
        c_mid = (c - c_hi).astype(BF16).astype(F32)
        c_lo = c - c_hi - c_mid

        def extras(base):
            return jnp.where(lane == base, c_hi, jnp.where(lane == base + 1, c_mid,
                             jnp.where(lane == base + 2, c_lo, 0.0)))

        ext_a, ext_b = extras(DA_HEAD_DIM), extras(0)
        ones_row = lax.broadcasted_iota(jnp.int32, (ATT_ROWS - DA_V_DIM, tq), 0) == 0
        for cblk in range(n_kv):
            rows = slice(cblk * tq, (cblk + 1) * tq)
            kblk = k_ref[0, rows, :].astype(F32)
            ka[rows, :] = jnp.where(lane < DA_HEAD_DIM, kblk, ext_a).astype(BF16)
            kb_[rows, :] = jnp.where(lane >= DA_HEAD_DIM, kblk, ext_b).astype(BF16)
            vt[0:DA_V_DIM, rows] = v_ref[0, rows, :].astype(F32).T.astype(BF16)
            vt[DA_V_DIM:, rows] = jnp.where(ones_row, 1.0, 0.0).astype(BF16)

    qt = q_ref[0].astype(F32).T
    row = lax.broadcasted_iota(jnp.int32, qt.shape, 0)
    qa = jnp.where(row < DA_HEAD_DIM, qt, jnp.where(row < DA_HEAD_DIM + 3, 1.0, 0.0)).astype(BF16)
    qb = jnp.where(row >= DA_HEAD_DIM, qt, jnp.where(row < 3, 1.0, 0.0)).astype(BF16)

    streams = ((ka, qa, acc1), (kb_, qb, acc2))

    def scores(start, rows, dst, lanes=slice(None)):
        for mi, (kref, qm, _) in enumerate(streams):
            dst[mi, :, lanes] = _dot(kref[pl.ds(start, rows), :], qm[:, lanes])

    def absorb(src, start, bias, shift, ms, lanes=slice(None)):
        v_t = vt[:, pl.ds(start, hk)]
        out = []
        for mi, ((_, _, acc), m_old) in enumerate(zip(streams, ms)):
            u = src[mi, :, lanes]
            m_old = m_old[:, lanes]
            if bias is not None:
                u = u + bias
            m_new = jnp.maximum(m_old, jnp.max(u, axis=0, keepdims=True) + shift)
            p = jnp.exp2(u - (m_new - shift))
            pv = _dot(v_t, p.astype(BF16))
            acc[:, lanes] = jnp.exp2(m_old - m_new) * acc[:, lanes] + pv
            out.append(m_new)
        return tuple(out)

    hk = tq // 2
    acc1[...] = jnp.zeros(acc1.shape, F32)
    acc2[...] = jnp.zeros(acc2.shape, F32)
    neg = jnp.full((1, tq), NEG_INF, F32)
    scores(0, hk, ua)

    def body(j, ms):
        s0 = pl.multiple_of(2 * j * hk, hk)
        shift = -slope2 * ((qi - j) * tq).astype(F32)
        scores(s0 + hk, hk, ub)
        ms = absorb(ua, s0, None, shift, ms)
        scores(s0 + 2 * hk, hk, ua)
        return absorb(ub, s0 + hk, None, shift, ms)

    ms = lax.fori_loop(0, qi, body, (neg, neg))
    diag = pl.multiple_of(qi * tq, tq)
    no_shift = jnp.zeros((1, 1), F32)
    late = slice(hk, tq)
    scores(diag + hk, hk, ub, late)
    early = slice(0, hk)
    m_early = absorb(ua, diag, bdiag[0:hk, early], no_shift, ms, early)
    m_late = absorb(ua, diag, None, no_shift, ms, late)
    ms = tuple(jnp.concatenate([a, b], axis=1) for a, b in zip(m_early, m_late))
    absorb(ub, diag + hk, bdiag[hk:tq, late], no_shift, ms, late)

    lam = (jnp.exp(jnp.sum(lq1_ref[...] * lk1_ref[...], axis=-1, keepdims=True))
           - jnp.exp(jnp.sum(lq2_ref[...] * lk2_ref[...], axis=-1, keepdims=True)) + lam0)
    o = (acc1[0:DA_V_DIM, :] / acc1[DA_V_DIM:DA_V_DIM + 1, :]
         - lam * (acc2[0:DA_V_DIM, :] / acc2[DA_V_DIM:DA_V_DIM + 1, :]))
    o = o * lax.rsqrt(jnp.mean(o * o, axis=0, keepdims=True) + EPS)
    o_ref[0] = (o.T * (gsub_ref[...] * (1.0 - lam0))).astype(BF16)


def _attention(proj3, slopes, lq1, lk1, lq2, lk2, g_subln, lam0, tq=1024):
    b, s, _ = proj3.shape
    kcol, vcol = COL_K // LANES, COL_V // LANES
    small = lambda shape: pl.BlockSpec(shape, lambda bi, h, qi: (0,) * len(shape))
    return pl.pallas_call(
        functools.partial(_attn_body, lam0, tq),
        grid=(b, DA_HEADS, s // tq),
        in_specs=[
            pl.BlockSpec((1, 1, LANES), lambda bi, h, qi: (h, 0, 0)),
            small((1, DA_HEAD_DIM)), small((1, DA_HEAD_DIM)),
            small((1, DA_HEAD_DIM)), small((1, DA_HEAD_DIM)),
            small((1, DA_V_DIM)),
            pl.BlockSpec((1, tq, LANES), lambda bi, h, qi: (bi, qi, h)),
            pl.BlockSpec((1, s, LANES), lambda bi, h, qi: (bi, 0, kcol + h)),
            pl.BlockSpec((1, s, LANES), lambda bi, h, qi: (bi, 0, vcol + h)),
        ],
        out_specs=pl.BlockSpec((1, tq, LANES), lambda bi, h, qi: (bi, qi, h)),
        out_shape=jax.ShapeDtypeStruct((b, s, DA_HEADS * DA_V_DIM), BF16),
        scratch_shapes=[
            pltpu.VMEM((tq, tq), F32),
            pltpu.VMEM((s, LANES), BF16), pltpu.VMEM((s, LANES), BF16),
            pltpu.VMEM((ATT_ROWS, s), BF16),
            pltpu.VMEM((ATT_ROWS, tq), F32), pltpu.VMEM((ATT_ROWS, tq), F32),
            pltpu.VMEM((2, tq // 2, tq), F32), pltpu.VMEM((2, tq // 2, tq), F32),
        ],
        compiler_params=_cparams(("arbitrary", "arbitrary", "arbitrary"), 40),
        name="diff_attention",
    )(slopes, lq1, lk1, lq2, lk2, g_subln, proj3, proj3, proj3)


def _split3(x):
    hi = x.astype(BF16)
    r = x - hi.astype(F32)
    mid = r.astype(BF16)
    lo = (r - mid.astype(F32)).astype(BF16)
    return hi, mid, lo


def _dot3_rhs(a_b, x):
    hi, mid, lo = _split3(x)
    return _dot(a_b, hi) + _dot(a_b, mid) + _dot(a_b, lo)


def _dot3_lhs(x, a_b):
    hi, mid, lo = _split3(x)
    return _dot(hi, a_b) + _dot(mid, a_b) + _dot(lo, a_b)


def _silu(x):
    return x * jax.nn.sigmoid(x)


def _ssd_body(ts, z_ref, xbc_ref, dt_ref, cw_ref, cb_ref, dtb_ref, alog_ref, dexp_ref, g_ref,
              tril_ref, ones_ref, expand_ref, eye_ref, trilrep_ref, bd_ref,
              o_ref, xe, state):
    ti = pl.program_id(1)

    @pl.when(ti == 0)
    def _():
        xe[0:8, :] = jnp.zeros((8, SSD_XBC), F32)
        state[...] = jnp.zeros(state.shape, F32)

    xe[8:8 + ts, :] = xbc_ref[0].astype(F32)
    conv = cb_ref[...] + cw_ref[0:1, :] * xe[5:5 + ts, :]
    for kk in range(1, SSD_CONV):
        conv = conv + cw_ref[kk:kk + 1, :] * xe[5 + kk:5 + kk + ts, :]
    tail = xe[ts:ts + 8, :]
    xe[0:8, :] = tail
    xc = _silu(conv)
    xs = xc[:, :D_MODEL]
    bm = xc[:, D_MODEL:D_MODEL + SSD_GROUPS * SSD_STATE].astype(BF16)
    cm = xc[:, D_MODEL + SSD_GROUPS * SSD_STATE:].astype(BF16)

    dt = jax.nn.softplus(dt_ref[0] + dtb_ref[...])
    a = -jnp.exp(alog_ref[...]) * dt
    acum = _dot3_rhs(tril_ref[...], a)
    colexp = _dot3_lhs(acum, expand_ref[...])
    dtexp = _dot3_lhs(dt, expand_ref[...])
    rowexp = _dot3_rhs(ones_ref[...], colexp * eye_ref[...])
    lmat = jnp.exp(jnp.where(trilrep_ref[...] > 0.5, colexp - rowexp, NEG_INF))
    xdt = xs * dtexp
    decay_in = jnp.exp(colexp)

    bdmask = bd_ref[...] > 0.5
    ys = []
    for c in range(ts // CHUNK):
        rows = slice(c * CHUNK, (c + 1) * CHUNK)
        a_last = colexp[(c + 1) * CHUNK - 1:(c + 1) * CHUNK, :]
        decay_st = jnp.exp(a_last - colexp[rows, :])
        chunk_decay = jnp.exp(a_last)
        xd = (xdt[rows, :] * decay_st).astype(BF16)
        xb = xdt[rows, :].astype(BF16)
        yg = []
        for g in range(SSD_GROUPS):
            lanes = slice(g * GROUP_LANES, (g + 1) * GROUP_LANES)
            cg = cm[rows, g * SSD_STATE:(g + 1) * SSD_STATE]
            bg = bm[rows, g * SSD_STATE:(g + 1) * SSD_STATE]
            cb = _dot_nt(cg, jnp.concatenate([bg] * 4, axis=0))
            w = (cb * lmat[rows, lanes]).astype(BF16)
            xg = xb[:, lanes]
            xbd = jnp.where(bdmask, jnp.concatenate([xg] * 4, axis=0), jnp.zeros((), BF16))
            y_diag = _dot(w, xbd)
            prev = state[g]
            y_off = _dot(cg, prev.astype(BF16)) * decay_in[rows, lanes]
            state[g] = prev * chunk_decay[:, lanes] + _dot_tn(bg, xd[:, lanes])
            yg.append(y_diag + y_off)
        ys.append(jnp.concatenate(yg, axis=1))
    y = jnp.concatenate(ys, axis=0) + dexp_ref[...] * xs
    y = y * _silu(z_ref[0].astype(F32))
    o_ref[0] = _rms(y, g_ref[...]).astype(BF16)


def _ssd_constants(ts):
    r = jnp.arange(ts)
    same_chunk = (r[:, None] // CHUNK) == (r[None, :] // CHUNK)
    tril = (same_chunk & (r[:, None] >= r[None, :])).astype(BF16)
    ones = same_chunk.astype(BF16)
    lane = jnp.arange(D_MODEL)
    expand = (jnp.arange(LANES)[:, None] == (lane[None, :] // SSD_HEAD_DIM)).astype(BF16)
    l_in = r[:, None] % CHUNK
    s_in = lane[None, :] % SSD_HEAD_DIM
    eye = (l_in == s_in).astype(F32)
    trilrep = (l_in >= s_in).astype(F32)
    q = jnp.arange(GROUP_LANES)
    bd = ((q[:, None] // SSD_HEAD_DIM) == (q[None, :] // SSD_HEAD_DIM)).astype(F32)
    return tril, ones, expand, eye, trilrep, bd


def _ssd(proj3, dt3, conv_w, conv_b, dt_bias, a_log, d_exp, g_ssd, ts=256):
    b, s, _ = proj3.shape
    consts = _ssd_constants(ts)
    full = lambda arr: pl.BlockSpec(arr.shape, lambda bi, ti: (0,) * arr.ndim)
    params = (conv_w, conv_b, dt_bias, a_log, d_exp, g_ssd)
    return pl.pallas_call(
        functools.partial(_ssd_body, ts),
        grid=(b, s // ts),
        in_specs=[
            pl.BlockSpec((1, ts, D_MODEL), lambda bi, ti: (bi, ti, COL_Z // D_MODEL)),
            pl.BlockSpec((1, ts, SSD_XBC), lambda bi, ti: (bi, ti, COL_XBC // SSD_XBC)),
            pl.BlockSpec((1, ts, LANES), lambda bi, ti: (bi, ti, 0)),
        ] + [full(p) for p in params] + [full(c) for c in consts],
        out_specs=pl.BlockSpec((1, ts, D_MODEL), lambda bi, ti: (bi, ti, 0)),
        out_shape=jax.ShapeDtypeStruct((b, s, D_MODEL), BF16),
        scratch_shapes=[
            pltpu.VMEM((ts + 8, SSD_XBC), F32),
            pltpu.VMEM((SSD_GROUPS, SSD_STATE, GROUP_LANES), F32),
        ],
        compiler_params=_cparams(("arbitrary", "arbitrary"), 48),
        name="ssd_mixer",
    )(proj3, proj3, dt3, *params, *consts)


def _merge_body(att_ref, ssd_ref, ga_ref, gb_ref, x_ref, wa_ref, wb_ref, wo_ref, o_ref):
    ya = _dot(att_ref[...], wa_ref[...])
    yb = _dot(ssd_ref[...], wb_ref[...])
    ga = jax.nn.sigmoid(ga_ref[...].astype(F32))
    gb = jax.nn.sigmoid(gb_ref[...].astype(F32))
    m = (ga * ya + gb * yb).astype(BF16)
    o_ref[...] = x_ref[...] + _dot(m, wo_ref[...])


def _merge(att2, ssd2, proj, x2, wa, wb, wo, tm=1024):
    t = x2.shape[0]
    row = lambda c: pl.BlockSpec((tm, D_MODEL), lambda i: (i, c))
    wspec = pl.BlockSpec((D_MODEL, D_MODEL), lambda i: (0, 0))
    gcol = COL_GATE // D_MODEL
    return pl.pallas_call(
        _merge_body,
        grid=(t // tm,),
        in_specs=[row(0), row(0), row(gcol), row(gcol + 1), row(0), wspec, wspec, wspec],
        out_specs=row(0),
        out_shape=jax.ShapeDtypeStruct((t, D_MODEL), F32),
        compiler_params=_cparams(("arbitrary",), 40),
        name="gated_merge",
    )(att2, ssd2, proj, proj, x2, wa, wb, wo)


SUBLANES = 8


def _batcher_network(n):
    def merge(lo, hi, r):
        step = r * 2
        if step < hi - lo:
            yield from merge(lo, hi, step)
            yield from merge(lo + r, hi, step)
            yield from [(i, i + r) for i in range(lo + r, hi - r, step)]
        else:
            yield (lo, lo + r)

    def sort(lo, hi):
        if hi - lo >= 1:
            mid = lo + (hi - lo) // 2
            yield from sort(lo, mid)
            yield from sort(mid + 1, hi)
            yield from merge(lo, hi, 1)

    return tuple(sort(0, n - 1))


_SORT16 = _batcher_network(PEER_TOPK)


def _exchange(xs, i, j):
    xs[i], xs[j] = jnp.maximum(xs[i], xs[j]), jnp.minimum(xs[i], xs[j])


def _all_sublanes(x, op):
    for shift in (4, 2, 1):
        x = op(x, pltpu.roll(x, shift, 0))
    return x


def _top16(s):
    xs = [s[SUBLANES * v:SUBLANES * (v + 1), :] for v in range(PEER_TOPK)]
    for i, j in _SORT16:
        _exchange(xs, i, j)
    for shift in (4, 2, 1):
        ys = [pltpu.roll(x, shift, 0) for x in xs]
        xs = [jnp.maximum(xs[k], ys[PEER_TOPK - 1 - k]) for k in range(PEER_TOPK)]
        for d in (8, 4, 2, 1):
            for i in range(PEER_TOPK):
                if i & d == 0:
                    _exchange(xs, i, i + d)
    return xs


def _stack(rows, sub):
    out = rows[SUBLANES - 1]
    for r in range(SUBLANES - 2, -1, -1):
        out = jnp.where(sub == r, rows[r], out)
    return out


def _candidate_groups(a_rep, a_hi, b_lo, b_hi, b0, combine):
    groups = [combine(a_rep[0], b_lo), combine(a_rep[0], b_hi)]
    groups += [combine(a_rep[a], b_lo) for a in range(1, SUBLANES)]
    groups.append(combine(a_hi, b0))
    return groups


def _route_body(x_ref, g_ref, wq_ref, keys_ref, ht_ref, a_ref, b_ref, tau_ref):
    hb = _rms(x_ref[...], g_ref[...])
    ht = hb.T.astype(BF16)
    ht_ref[...] = ht
    qt = _dot(wq_ref[...], ht).astype(BF16)
    tt = ht.shape[1]
    sub = lax.broadcasted_iota(jnp.int32, (SUBLANES, tt), 0)
    for n in range(PEER_HEADS):
        s1 = _dot(keys_ref[2 * n], qt[(2 * n) * PEER_HALF:(2 * n + 1) * PEER_HALF, :])
        s2 = _dot(keys_ref[2 * n + 1], qt[(2 * n + 1) * PEER_HALF:(2 * n + 2) * PEER_HALF, :])
        r1 = _top16(s1)
        r2 = _top16(s2)
        add = lambda p, q: p + q
        cand = _candidate_groups(r1, _stack(r1[SUBLANES:], sub), _stack(r2[:SUBLANES], sub),
                                 _stack(r2[SUBLANES:], sub), r2[0], add)
        work = cand
        for it in range(PEER_TOPK):
            m = functools.reduce(jnp.maximum, work)
            m = _all_sublanes(m, jnp.maximum)
            if it < PEER_TOPK - 1:
                work = [jnp.where(w == m, NEG_INF, w) for w in work]
        sel = [c >= m for c in cand]
        top = r1[0] + r2[0]
        z = functools.reduce(add, [jnp.where(s, jnp.exp(c - top), 0.0) for s, c in zip(sel, cand)])
        inv_z = GELU_OUT_SCALE / _all_sublanes(z, add)
        a_ref[n] = jnp.exp(s1 - r1[0][0:1]) * inv_z[0:1]
        b_ref[n] = jnp.exp(s2 - r2[0][0:1])
        pa = [jnp.exp(r - r1[0]) * inv_z for r in r1]
        pb = [jnp.exp(r - r2[0]) for r in r2]
        prod = _candidate_groups(pa, _stack(pa[SUBLANES:], sub), _stack(pb[:SUBLANES], sub),
                                 _stack(pb[SUBLANES:], sub), pb[0], lambda p, q: p * q)
        tau = functools.reduce(jnp.minimum, [jnp.where(s, p, jnp.inf) for s, p in zip(sel, prod)])
        tau_ref[n] = _all_sublanes(tau, jnp.minimum)[0:1]


def _route(x1, g_ffn, wq_t, keys, tt=512):
    t = x1.shape[0]
    return pl.pallas_call(
        _route_body,
        grid=(t // tt,),
        in_specs=[
            pl.BlockSpec((tt, D_MODEL), lambda i: (i, 0)),
            pl.BlockSpec((1, D_MODEL), lambda i: (0, 0)),
            pl.BlockSpec(wq_t.shape, lambda i: (0, 0)),
            pl.BlockSpec(keys.shape, lambda i: (0, 0, 0)),
        ],
        out_specs=[
            pl.BlockSpec((D_MODEL, tt), lambda i: (0, i)),
            pl.BlockSpec((PEER_HEADS, PEER_NKEYS, tt), lambda i: (0, 0, i)),
            pl.BlockSpec((PEER_HEADS, PEER_NKEYS, tt), lambda i: (0, 0, i)),
            pl.BlockSpec((PEER_HEADS, 1, tt), lambda i: (0, 0, i)),
        ],
        out_shape=[
            jax.ShapeDtypeStruct((D_MODEL, t), BF16),
            jax.ShapeDtypeStruct((PEER_HEADS, PEER_NKEYS, t), F32),
            jax.ShapeDtypeStruct((PEER_HEADS, PEER_NKEYS, t), F32),
            jax.ShapeDtypeStruct((PEER_HEADS, 1, t), F32),
        ],
        compiler_params=_cparams(("arbitrary",), 48),
        name="peer_route",
    )(x1, g_ffn, wq_t, keys)


def _experts_body(te, ht_ref, u_ref, vt_ref, a_ref, b_ref, tau_ref, x_ref, g_ref, o_ref, acc, gates, act):
    e = pl.program_id(1)
    nblk = te // PEER_NKEYS
    tt = ht_ref.shape[1]

    @pl.when(e == 0)
    def _():
        acc[...] = jnp.zeros(acc.shape, F32)

    a_rows = [a_ref[n] for n in range(PEER_HEADS)]
    for ib in range(nblk):
        rows = slice(ib * PEER_NKEYS, (ib + 1) * PEER_NKEYS)
        for lb in range(tt // LANES):
            lanes = slice(lb * LANES, (lb + 1) * LANES)
            gate = None
            for n in range(PEER_HEADS):
                p = a_rows[n][ib:ib + 1, lanes] * b_ref[n, :, lanes]
                term = jnp.where(p >= tau_ref[n, :, lanes], p, 0.0)
                gate = term if gate is None else gate + term
            gates[rows, lanes] = gate
    s = _dot(u_ref[...], ht_ref[...])
    act[...] = (gates[...] * (s * (1.0 + lax.erf(s)))).astype(BF16)
    acc[...] += _dot(vt_ref[...], act[...])

    @pl.when(e == pl.num_programs(1) - 1)
    def _():
        o_ref[...] = _rms(x_ref[...] + acc[...].T, g_ref[...])


def _experts(h2t, u_b, vt_b, a_f, b_f, tau, x1, g_final, tt=1024, te=1024):
    t = x1.shape[0]
    n_e = PEER_EXPERTS // te
    return pl.pallas_call(
        functools.partial(_experts_body, te),
        grid=(t // tt, n_e),
        in_specs=[
            pl.BlockSpec((D_MODEL, tt), lambda i, e: (0, i)),
            pl.BlockSpec((te, D_MODEL), lambda i, e: (e, 0)),
            pl.BlockSpec((D_MODEL, te), lambda i, e: (0, e)),
            pl.BlockSpec((PEER_HEADS, te // PEER_NKEYS, tt), lambda i, e: (0, e, i)),
            pl.BlockSpec((PEER_HEADS, PEER_NKEYS, tt), lambda i, e: (0, 0, i)),
            pl.BlockSpec((PEER_HEADS, 1, tt), lambda i, e: (0, 0, i)),
            pl.BlockSpec((tt, D_MODEL), lambda i, e: (i, 0)),
            pl.BlockSpec((1, D_MODEL), lambda i, e: (0, 0)),
        ],
        out_specs=pl.BlockSpec((tt, D_MODEL), lambda i, e: (i, 0)),
        out_shape=jax.ShapeDtypeStruct((t, D_MODEL), F32),
        scratch_shapes=[pltpu.VMEM((D_MODEL, tt), F32), pltpu.VMEM((te, tt), F32),
                        pltpu.VMEM((te, tt), BF16)],
        compiler_params=_cparams(("arbitrary", "arbitrary"), 56),
        name="peer_experts",
    )(h2t, u_b, vt_b, a_f, b_f, tau, x1, g_final)


def _lambda_init(layer_index):
    return 0.8 - 0.6 * math.exp(-0.3 * (layer_index - 1))


def _pad_lanes(v):
    return jnp.pad(v.reshape(1, -1), ((0, 0), (0, LANES - v.shape[-1])))


def _layer(x, layer, g_mix, w_in, lam_q1, lam_k1, lam_q2, lam_k2, g_subln, conv_w, conv_b, dt_bias,
           a_log, d_skip, g_ssd, w_branch_a, w_branch_b, w_out, g_ffn, w_query, sub_keys,
           expert_u, expert_v, g_out):
    b, s, d = x.shape
    t = b * s
    x2 = x.reshape(t, d)
    row = lambda v: v.reshape(1, -1)

    w = w_in[layer]
    dt_lo = COL_GATE
    q_scale = DA_HEAD_DIM ** -0.5 * LOG2E
    w_main = jnp.concatenate([w[:, :COL_K] * q_scale, w[:, COL_K:dt_lo], w[:, dt_lo + SSD_HEADS:]],
                             axis=1).astype(BF16)
    w_dt = jnp.pad(w[:, dt_lo:dt_lo + SSD_HEADS], ((0, 0), (0, LANES - SSD_HEADS))).astype(BF16)
    slopes = jnp.exp2(-8.0 * jnp.arange(1, DA_HEADS + 1, dtype=F32) / DA_HEADS)
    slopes = jnp.broadcast_to(slopes[:, None, None], (DA_HEADS, 1, LANES))

    proj, dt_raw = _inproj(x2, row(g_mix[layer]), w_main, w_dt)
    proj3 = proj.reshape(b, s, N_MAIN)

    att = _attention(proj3, slopes, row(lam_q1[layer]), row(lam_k1[layer]), row(lam_q2[layer]),
                     row(lam_k2[layer]), row(g_subln[layer]), _lambda_init(layer + 1))
    ssd = _ssd(proj3, dt_raw.reshape(b, s, LANES), conv_w[layer].reshape(SSD_CONV, SSD_XBC),
               row(conv_b[layer]), _pad_lanes(dt_bias[layer]), _pad_lanes(a_log[layer]),
               row(jnp.repeat(d_skip[layer], SSD_HEAD_DIM)), row(g_ssd[layer]))

    x1 = _merge(att.reshape(t, d), ssd.reshape(t, d), proj, x2, w_branch_a[layer].astype(BF16),
                w_branch_b[layer].astype(BF16), w_out[layer].astype(BF16))

    keys = sub_keys[layer].reshape(2 * PEER_HEADS, PEER_NKEYS, PEER_HALF).astype(BF16)
    h2t, a_f, b_f, tau = _route(x1, row(g_ffn[layer]), w_query[layer].T.astype(BF16), keys)
    out = _experts(h2t, (expert_u[layer] * GELU_ARG_SCALE).astype(BF16), expert_v[layer].T.astype(BF16), a_f, b_f, tau,
                   x1, row(g_out))
    return out.reshape(b, s, d)


def kernel(x, g_mix, w_in, lam_q1, lam_k1, lam_q2, lam_k2, g_subln, conv_w, conv_b, dt_bias, a_log,
           d_skip, g_ssd, w_branch_a, w_branch_b, w_out, g_ffn, w_query, sub_keys, expert_u, expert_v,
           g_final):
    depth = g_mix.shape[0]
    assert depth == 1, "the final RMSNorm is fused into the last (only) layer's expert kernel"
    return _layer(x, 0, g_mix, w_in, lam_q1, lam_k1, lam_q2, lam_k2, g_subln, conv_w, conv_b, dt_bias,
                  a_log, d_skip, g_ssd, w_branch_a, w_branch_b, w_out, g_ffn, w_query, sub_keys,
                  expert_u, expert_v, g_final)
```
